```python
import math
import jax
import jax.numpy as jnp
from jax import lax
import numpy as np


D_MODEL = 1024
BATCH = 8
SEQ = 4096
DEPTH = 2

GRID_W = 64
CTX_LEN = 256
EPS = 1e-6
ROPE_BASE = 10000.0

BR_WIDTH = D_MODEL // 2
A_DK = 128
A_DV = 128
A_HEADS = BR_WIDTH // A_DK
A_WIDTH = A_HEADS * A_DV
A_CONV = 5
A_CHUNK = 64
B_HD = 64
B_Q_HEADS = BR_WIDTH // B_HD
B_KV_HEADS = B_Q_HEADS // 4
B_WIDTH = B_Q_HEADS * B_HD
WINDOW = 128
B_BLOCK = 128
C_HD = 128
C_HEADS = BR_WIDTH // C_HD
C_WIDTH = C_HEADS * C_HD
C_CHUNK = 64
N_BRANCH = 3

SPLIT_SIZES = (3 * A_WIDTH, A_WIDTH, 2 * A_HEADS, 2 * A_HEADS,
               B_Q_HEADS * B_HD, 2 * B_KV_HEADS * B_HD, B_WIDTH,
               3 * C_WIDTH, C_WIDTH, N_BRANCH * D_MODEL)
SPLIT_POINTS = tuple(int(v) for v in np.cumsum(SPLIT_SIZES)[:-1])
IN_WIDTH = int(sum(SPLIT_SIZES))

kernel_name = 'hybrid_delta_window_retention_dit'


def rms_norm(x, w):
    xf = x.astype(jnp.float32)
    y = xf * lax.rsqrt(jnp.mean(xf * xf, axis=-1, keepdims=True) + EPS)
    return (y * w.astype(jnp.float32)).astype(x.dtype)


def l2_normalize(x):
    xf = x.astype(jnp.float32)
    return xf * lax.rsqrt(jnp.sum(xf * xf, axis=-1, keepdims=True) + EPS)


def to_heads(x, h):
    b, t, _ = x.shape
    return x.reshape(b, t, h, -1).transpose(0, 2, 1, 3)


def from_heads(x):
    b, h, t, d = x.shape
    return x.transpose(0, 2, 1, 3).reshape(b, t, h * d)


def flip_t(a):
    return jnp.flip(a, axis=2)


def rope_angles(pos, n_freq):
    inv = ROPE_BASE ** (-jnp.arange(n_freq, dtype=jnp.float32) / n_freq)
    return pos[:, None] * inv[None, :]


def apply_rot(x, ang):
    x1, x2 = jnp.split(x, 2, axis=-1)
    cos = jnp.cos(ang).astype(x.dtype)
    sin = jnp.sin(ang).astype(x.dtype)
    return jnp.concatenate([x1 * cos - x2 * sin, x1 * sin + x2 * cos], axis=-1)


def apply_axial(x, ang_r, ang_c):
    half = x.shape[-1] // 2
    return jnp.concatenate([apply_rot(x[..., :half], ang_r), apply_rot(x[..., half:], ang_c)], axis=-1)


def short_conv(x, w):
    k = w.shape[0]
    p = k // 2
    return lax.conv_general_dilated(x, w[:, None, :].astype(x.dtype), window_strides=(1,),
                                    padding=[(p, p)], dimension_numbers=('NWC', 'WIO', 'NWC'),
                                    feature_group_count=x.shape[-1])


def gated_delta_chunk(q, k, v, g, beta, s0):
    b, h, t, dk = q.shape
    dv = v.shape[-1]
    c = A_CHUNK
    n = t // c
    q = q.reshape(b, h, n, c, dk)
    k = k.reshape(b, h, n, c, dk)
    v = v.reshape(b, h, n, c, dv)
    g = jnp.cumsum(g.reshape(b, h, n, c), axis=-1)
    beta = beta.reshape(b, h, n, c)
    incl = jnp.tril(jnp.ones((c, c), dtype=bool))
    strict = jnp.tril(jnp.ones((c, c), dtype=bool), -1)
    decay = jnp.exp(jnp.where(incl, g[..., :, None] - g[..., None, :], -jnp.inf))
    kb = k * beta[..., None]
    lmat = jnp.where(strict, jnp.einsum('bhnid,bhnjd->bhnij', kb, k) * decay, 0.0)
    eye = jnp.eye(c, dtype=jnp.float32)
    tinv = lax.linalg.triangular_solve(lmat + eye, jnp.broadcast_to(eye, lmat.shape),
                                       left_side=True, lower=True)
    u = jnp.einsum('bhnij,bhnjd->bhnid', tinv, v * beta[..., None])
    w = jnp.einsum('bhnij,bhnjd->bhnid', tinv, kb * jnp.exp(g)[..., None])
    qk = jnp.einsum('bhnid,bhnjd->bhnij', q, k) * decay
    q_dec = q * jnp.exp(g)[..., None]
    g_last = g[..., -1]
    k_tail = k * jnp.exp(g_last[..., None] - g)[..., None]

    def step(state, inp):
        u_n, w_n, qk_n, qd_n, kt_n, gl_n = inp
        v_new = u_n - jnp.einsum('bhcd,bhde->bhce', w_n, state)
        o_n = jnp.einsum('bhcd,bhde->bhce', qd_n, state) + jnp.einsum('bhij,bhje->bhie', qk_n, v_new)
        state = state * jnp.exp(gl_n)[..., None, None] + jnp.einsum('bhcd,bhce->bhde', kt_n, v_new)
        return state, o_n

    mv = lambda a: jnp.moveaxis(a, 2, 0)
    s_fin, o = lax.scan(step, s0, (mv(u), mv(w), mv(qk), mv(q_dec), mv(k_tail), mv(g_last)))
    return jnp.moveaxis(o, 0, 2).reshape(b, h, t, dv), s_fin


def delta_prep(qkv, b_raw, a_raw, conv_w, a_log, dt_bias):
    qkv = jax.nn.silu(short_conv(qkv, conv_w))
    q, k, v = jnp.split(qkv, 3, axis=-1)
    q = l2_normalize(to_heads(q, A_HEADS)) * (A_DK ** -0.5)
    k = l2_normalize(to_heads(k, A_HEADS))
    v = to_heads(v, A_HEADS).astype(jnp.float32)
    beta = jax.nn.sigmoid(b_raw.astype(jnp.float32)).transpose(0, 2, 1)
    g = -jnp.exp(a_log.astype(jnp.float32))[None, :, None] * jax.nn.softplus(
        (a_raw.astype(jnp.float32) + dt_bias.astype(jnp.float32)).transpose(0, 2, 1))
    return q, k, v, beta, g


def bidir_delta(q, k, v, beta, g, s0_f, s0_b):
    h = A_HEADS
    o_f, s_f = gated_delta_chunk(q, k, v, g[:, :h], beta[:, :h], s0_f)
    o_b, s_b = gated_delta_chunk(flip_t(q), flip_t(k), flip_t(v), flip_t(g[:, h:]), flip_t(beta[:, h:]), s0_b)
    return o_f + flip_t(o_b), s_f, s_b


def gated_head_rmsnorm(o, z, w):
    o = o * lax.rsqrt(jnp.mean(o * o, axis=-1, keepdims=True) + EPS) * w.astype(jnp.float32)
    return (from_heads(o) * jax.nn.silu(z.astype(jnp.float32))).astype(z.dtype)


def window_attention(q, k, v, k_ctx, v_ctx, sink):
    b, t, hq, dh = q.shape
    hkv = k.shape[2]
    grp = hq // hkv
    blk = B_BLOCK
    nb = t // blk
    l = k_ctx.shape[1]
    scale = dh ** -0.5
    qb = q.reshape(b, nb, blk, hkv, grp, dh)
    pad = ((0, 0), (blk, blk), (0, 0), (0, 0))
    kp = jnp.pad(k, pad).reshape(b, nb + 2, blk, hkv, dh)
    vp = jnp.pad(v, pad).reshape(b, nb + 2, blk, hkv, dh)
    kw = jnp.concatenate([kp[:, :-2], kp[:, 1:-1], kp[:, 2:]], axis=2)
    vw = jnp.concatenate([vp[:, :-2], vp[:, 1:-1], vp[:, 2:]], axis=2)
    s_win = jnp.einsum('bnqhgd,bnkhd->bnhgqk', qb, kw).astype(jnp.float32) * scale
    s_ctx = jnp.einsum('bnqhgd,blhd->bnhgql', qb, k_ctx).astype(jnp.float32) * scale
    rel = jnp.arange(3 * blk)[None, :] - blk - jnp.arange(blk)[:, None]
    kpos = jnp.arange(nb)[:, None] * blk - blk + jnp.arange(3 * blk)[None, :]
    valid = (jnp.abs(rel) <= WINDOW)[None] & ((kpos >= 0) & (kpos < t))[:, None, :]
    s_win = jnp.where(valid[None, :, None, None], s_win, -jnp.inf)
    sink_s = jnp.broadcast_to(sink.astype(jnp.float32).reshape(1, 1, hkv, grp, 1, 1), s_win.shape[:-1] + (1,))
    p = jax.nn.softmax(jnp.concatenate([sink_s, s_ctx, s_win], axis=-1), axis=-1)
    p_ctx = p[..., 1:1 + l].astype(v.dtype)
    p_win = p[..., 1 + l:].astype(v.dtype)
    o = (jnp.einsum('bnhgql,blhd->bnqhgd', p_ctx, v_ctx)
         + jnp.einsum('bnhgqk,bnkhd->bnqhgd', p_win, vw))
    return o.reshape(b, t, hq * dh)


def context_attention(q, k, v, sink):
    b, l, hq, dh = q.shape
    hkv = k.shape[2]
    grp = hq // hkv
    qg = q.reshape(b, l, hkv, grp, dh)
    s = jnp.einsum('bqhgd,bkhd->bhgqk', qg, k).astype(jnp.float32) * (dh ** -0.5)
    sink_s = jnp.broadcast_to(sink.astype(jnp.float32).reshape(1, hkv, grp, 1, 1), s.shape[:-1] + (1,))
    p = jax.nn.softmax(jnp.concatenate([sink_s, s], axis=-1), axis=-1)[..., 1:]
    o = jnp.einsum('bhgqk,bkhd->bqhgd', p.astype(v.dtype), v)
    return o.reshape(b, l, hq * dh)


def retention_chunk(q, k, v, log_gamma, s0):
    b, h, t, dk = q.shape
    dv = v.shape[-1]
    c = C_CHUNK
    n = t // c
    q = q.reshape(b, h, n, c, dk)
    k = k.reshape(b, h, n, c, dk)
    v = v.reshape(b, h, n, c, dv)
    idx = jnp.arange(c, dtype=jnp.float32)
    lg = log_gamma.astype(jnp.float32)[:, None]
    incl = jnp.tril(jnp.ones((c, c), dtype=bool))
    dmask = jnp.exp(jnp.where(incl, (idx[:, None] - idx[None, :]) * lg[:, :, None], -jnp.inf))
    o_inner = jnp.einsum('bhnij,bhnje->bhnie',
                         jnp.einsum('bhnid,bhnjd->bhnij', q, k) * dmask[None, :, None], v)
    q_dec = q * jnp.exp((idx + 1.0) * lg)[None, :, None, :, None]
    k_dec = k * jnp.exp((c - 1.0 - idx) * lg)[None, :, None, :, None]
    chunk_dec = jnp.exp(c * lg)[None, :, :, None]

    def step(state, inp):
        qd_n, kd_n, v_n = inp
        o_n = jnp.einsum('bhcd,bhde->bhce', qd_n, state)
        state = state * chunk_dec + jnp.einsum('bhcd,bhce->bhde', kd_n, v_n)
        return state, o_n

    mv = lambda a: jnp.moveaxis(a, 2, 0)
    s_fin, o_cross = lax.scan(step, s0, (mv(q_dec), mv(k_dec), mv(v)))
    o = o_inner + jnp.moveaxis(o_cross, 0, 2)
    return o.reshape(b, h, t, dv), s_fin


def retention_prep(qkv, ang):
    q, k, v = jnp.split(qkv, 3, axis=-1)
    q = to_heads(q, C_HEADS).astype(jnp.float32)
    k = to_heads(k, C_HEADS).astype(jnp.float32) * (C_HD ** -0.5)
    v = to_heads(v, C_HEADS).astype(jnp.float32)
    if ang is not None:
        q = apply_rot(q, ang)
        k = apply_rot(k, ang)
    return q, k, v


def bidir_retention(q, k, v, log_gamma, s0_f, s0_b):
    h = C_HEADS
    o_f, s_f = retention_chunk(q, k, v, log_gamma[:h], s0_f)
    o_b, s_b = retention_chunk(flip_t(q), flip_t(k), flip_t(v), log_gamma[h:], s0_b)
    return o_f + flip_t(o_b), s_f, s_b


def retention_out(o, z, w):
    mu = jnp.mean(o, axis=-1, keepdims=True)
    var = jnp.mean(jnp.square(o - mu), axis=-1, keepdims=True)
    o = from_heads((o - mu) * lax.rsqrt(var + EPS)) * w.astype(jnp.float32)
    return (o * jax.nn.silu(z.astype(jnp.float32))).astype(z.dtype)


def merge_branches(y_a, y_b, y_c, logits, w_br, w_o):
    g_a, g_b, g_c = jnp.split(jax.nn.sigmoid(logits), 3, axis=-1)
    merged = g_a * (y_a @ w_br[0]) + g_b * (y_b @ w_br[1]) + g_c * (y_c @ w_br[2])
    return merged @ w_o


def setup_inputs(seed: int = 0) -> dict:
    key = jax.random.key(seed)
    ks = jax.random.split(key, 18)
    f32 = jnp.float32

    def nrm(k, shape, s):
        return jax.random.normal(k, shape, f32) * s

    x = nrm(ks[0], (BATCH, SEQ, D_MODEL), 1.0)
    c = nrm(ks[1], (BATCH, D_MODEL), 1.0)
    ctx = nrm(ks[2], (BATCH, CTX_LEN, D_MODEL), 1.0)
    c_ctx = nrm(ks[3], (D_MODEL,), 1.0)
    w_ada = nrm(ks[4], (DEPTH, D_MODEL, 3 * D_MODEL), 0.5 * D_MODEL ** -0.5)
    b_ada = nrm(ks[5], (DEPTH, 3 * D_MODEL), 0.02)
    norm_w = 1.0 + nrm(ks[6], (DEPTH, D_MODEL), 0.02)
    w_in = nrm(ks[7], (DEPTH, D_MODEL, IN_WIDTH), D_MODEL ** -0.5)
    a_conv_w = nrm(ks[8], (DEPTH, A_CONV, 3 * A_WIDTH), A_CONV ** -0.5)
    a_log = jnp.log(jax.random.uniform(ks[9], (DEPTH, 2 * A_HEADS), f32, 1.0, 16.0))
    dt = jnp.exp(jax.random.uniform(ks[10], (DEPTH, 2 * A_HEADS), f32, math.log(1e-3), math.log(1e-1)))
    a_dt_bias = dt + jnp.log(-jnp.expm1(-dt))
    a_norm_w = 1.0 + nrm(ks[11], (DEPTH, A_DV), 0.02)
    b_sink = nrm(ks[12], (DEPTH, B_Q_HEADS), 0.5)
    gam = 1.0 - 2.0 ** (-5.0 - np.arange(C_HEADS, dtype=np.float32))
    base = jnp.asarray(np.log(gam) - np.log1p(-gam), f32)
    c_decay = jnp.tile(base, 2)[None, :] + nrm(ks[13], (DEPTH, 2 * C_HEADS), 0.1)
    c_norm_w = 1.0 + nrm(ks[14], (DEPTH, C_WIDTH), 0.02)
    w_branch = nrm(ks[15], (DEPTH, N_BRANCH, BR_WIDTH, D_MODEL), BR_WIDTH ** -0.5)
    w_out = nrm(ks[16], (DEPTH, D_MODEL, D_MODEL), D_MODEL ** -0.5)
    final_norm_w = 1.0 + nrm(ks[17], (D_MODEL,), 0.02)
    return {'x': x, 'c': c, 'ctx': ctx, 'c_ctx': c_ctx, 'w_ada': w_ada, 'b_ada': b_ada,
            'norm_w': norm_w, 'w_in': w_in, 'a_conv_w': a_conv_w, 'a_log': a_log,
            'a_dt_bias': a_dt_bias, 'a_norm_w': a_norm_w, 'b_sink': b_sink, 'c_decay': c_decay,
            'c_norm_w': c_norm_w, 'w_branch': w_branch, 'w_out': w_out, 'final_norm_w': final_norm_w}


def reference(x, c, ctx, c_ctx, w_ada, b_ada, norm_w, w_in, a_conv_w, a_log, a_dt_bias, a_norm_w,
              b_sink, c_decay, c_norm_w, w_branch, w_out, final_norm_w):
    b, t, d = x.shape
    l = ctx.shape[1]
    rows_n = t // GRID_W
    rows = jnp.repeat(jnp.arange(rows_n, dtype=jnp.float32), GRID_W)
    cols = jnp.tile(jnp.arange(GRID_W, dtype=jnp.float32), rows_n)
    n_ax = B_HD // 4
    ang_r = rope_angles(rows, n_ax)[:, None, :]
    ang_c = rope_angles(cols, n_ax)[:, None, :]
    ang_ret = rope_angles(jnp.arange(t, dtype=jnp.float32), C_HD // 2)
    zeros_a = jnp.zeros((b, A_HEADS, A_DK, A_DV), jnp.float32)
    zeros_c = jnp.zeros((b, C_HEADS, C_HD, C_HD), jnp.float32)

    for layer in range(DEPTH):
        mod = jax.nn.silu(c) @ w_ada[layer] + b_ada[layer]
        shift, scale, gate = [m[:, None, :] for m in jnp.split(mod, 3, axis=-1)]
        mod_c = jax.nn.silu(c_ctx) @ w_ada[layer] + b_ada[layer]
        shift_c, scale_c, gate_c = jnp.split(mod_c, 3, axis=-1)
        h = rms_norm(x, norm_w[layer]) * (1.0 + scale) + shift
        hc = rms_norm(ctx, norm_w[layer]) * (1.0 + scale_c) + shift_c
        (a_qkv, a_z, a_beta, a_alpha, b_q, b_kv, b_z, c_qkv, c_z, merge) = jnp.split(
            h @ w_in[layer], SPLIT_POINTS, axis=-1)
        (a_qkv_c, a_z_c, a_beta_c, a_alpha_c, b_q_c, b_kv_c, b_z_c, c_qkv_c, c_z_c, merge_c) = jnp.split(
            hc @ w_in[layer], SPLIT_POINTS, axis=-1)

        qa, ka, va, beta_a, g_a = delta_prep(a_qkv, a_beta, a_alpha, a_conv_w[layer], a_log[layer], a_dt_bias[layer])
        qa_c, ka_c, va_c, beta_ac, g_ac = delta_prep(a_qkv_c, a_beta_c, a_alpha_c, a_conv_w[layer],
                                                     a_log[layer], a_dt_bias[layer])
        o_ac, s_af, s_ab = bidir_delta(qa_c, ka_c, va_c, beta_ac, g_ac, zeros_a, zeros_a)
        o_a, _, _ = bidir_delta(qa, ka, va, beta_a, g_a, s_af, s_ab)
        y_a = gated_head_rmsnorm(o_a, a_z, a_norm_w[layer]).astype(x.dtype)

        q_b = apply_axial(b_q.reshape(b, t, B_Q_HEADS, B_HD), ang_r, ang_c)
        k_b, v_b = jnp.split(b_kv.reshape(b, t, 2 * B_KV_HEADS, B_HD), 2, axis=2)
        k_b = apply_axial(k_b, ang_r, ang_c)
        q_bc = b_q_c.reshape(b, l, B_Q_HEADS, B_HD)
        k_bc, v_bc = jnp.split(b_kv_c.reshape(b, l, 2 * B_KV_HEADS, B_HD), 2, axis=2)
        y_b = (window_attention(q_b, k_b, v_b, k_bc, v_bc, b_sink[layer]) * jax.nn.silu(b_z)).astype(x.dtype)

        log_gamma = jax.nn.log_sigmoid(c_decay[layer].astype(jnp.float32))
        q_c, k_c, v_c = retention_prep(c_qkv, ang_ret)
        q_cc, k_cc, v_cc = retention_prep(c_qkv_c, None)
        o_cc, s_cf, s_cb = bidir_retention(q_cc, k_cc, v_cc, log_gamma, zeros_c, zeros_c)
        o_c, _, _ = bidir_retention(q_c, k_c, v_c, log_gamma, s_cf, s_cb)
        y_c = retention_out(o_c, c_z, c_norm_w[layer]).astype(x.dtype)

        out = merge_branches(y_a, y_b, y_c, merge, w_branch[layer], w_out[layer])
        x_new = x + gate * out

        if layer < DEPTH - 1:
            y_ac = gated_head_rmsnorm(o_ac, a_z_c, a_norm_w[layer]).astype(ctx.dtype)
            y_bc = (context_attention(q_bc, k_bc, v_bc, b_sink[layer]) * jax.nn.silu(b_z_c)).astype(ctx.dtype)
            y_cc = retention_out(o_cc, c_z_c, c_norm_w[layer]).astype(ctx.dtype)
            out_c = merge_branches(y_ac, y_bc, y_cc, merge_c, w_branch[layer], w_out[layer])
            ctx = ctx + gate_c * out_c
        x = x_new

    return rms_norm(x, final_norm_w)
```

```python
import functools

import jax
import jax.numpy as jnp
import numpy as np
from jax import lax
from jax.experimental import pallas as pl
from jax.experimental.pallas import tpu as pltpu

F32 = jnp.float32
BF16 = jnp.bfloat16

D_MODEL = 1024
GRID_W = 64
EPS = 1e-6
ROPE_BASE = 10000.0
BR_WIDTH = D_MODEL // 2
A_DK = 128
A_HEADS = BR_WIDTH // A_DK
A_WIDTH = A_HEADS * A_DK
A_CONV = 5
A_CHUNK = 64
B_HD = 64
B_Q_HEADS = BR_WIDTH // B_HD
B_KV_HEADS = B_Q_HEADS // 4
B_BLOCK = 128
C_HD = 128
C_HEADS = BR_WIDTH // C_HD
N_BRANCH = 3

LANES = 128
GROUP = 256
VMEM_LIMIT = 56 * 1024 * 1024


def _cparams(sem):
    return pltpu.CompilerParams(dimension_semantics=sem, vmem_limit_bytes=VMEM_LIMIT)


def _const_spec(shape):
    nd = len(shape)
    return pl.BlockSpec(shape, lambda *_: (0,) * nd, pipeline_mode=pl.Buffered(1))


def _ada_kernel(c_ref, w_ref, b_ref, o_ref):
    c = c_ref[...]
    s = c * jax.nn.sigmoid(c)
    o_ref[...] = jnp.dot(s, w_ref[...], preferred_element_type=F32,
                         precision=lax.Precision.HIGHEST) + b_ref[...]


def ada_mod(cc, w_ada, b_ada):
    depth = w_ada.shape[0]
    r, d = cc.shape
    tn = 1024
    nt = (3 * d) // tn
    return pl.pallas_call(
        _ada_kernel,
        grid=(depth, nt),
        in_specs=[pl.BlockSpec((r, d), lambda l, j: (0, 0)),
                  pl.BlockSpec((None, d, tn), lambda l, j: (l, 0, j)),
                  pl.BlockSpec((None, 1, tn), lambda l, j: (l, 0, j))],
        out_specs=pl.BlockSpec((None, r, tn), lambda l, j: (l, 0, j)),
        out_shape=jax.ShapeDtypeStruct((depth, r, 3 * d), F32),
        compiler_params=_cparams(("arbitrary", "arbitrary")),
        name="ada_mod",
    )(cc, w_ada, b_ada.reshape(depth, 1, 3 * d))


def _rot_axial(x, cosa, sinm, sinp):
    w = x.shape[-1]
    reps = w // LANES
    ca = jnp.concatenate([cosa] * reps, axis=-1)
    sm = jnp.concatenate([sinm] * reps, axis=-1)
    sp = jnp.concatenate([sinp] * reps, axis=-1)
    return x * ca + pltpu.roll(x, w - 16, 1) * sm + pltpu.roll(x, 16, 1) * sp


def _rot_half128(x, cosr, sinr):
    outs = []
    for t in range(x.shape[-1] // LANES):
        xt = x[:, t * LANES:(t + 1) * LANES]
        outs.append(xt * cosr + pltpu.roll(xt, LANES // 2, 1) * sinr)
    return jnp.concatenate(outs, axis=-1)


def _inproj_kernel(x_ref, mod_ref, nw_ref, cosa_ref, sinm_ref, sinp_ref, cosr_ref, sinr_ref,
                   w_aqkv, w_az, w_ba, w_bq, w_bkv, w_bz, w_cqkv, w_cz, w_mg,
                   o_aqkv, o_az, o_ba, o_bq, o_bk, o_bv, o_bz, o_cq, o_ck, o_cv, o_cz, o_mg):
    d = D_MODEL
    x = x_ref[...]
    y = x * lax.rsqrt(jnp.mean(x * x, axis=-1, keepdims=True) + EPS) * nw_ref[...]
    mod = mod_ref[...]
    h = (y * (1.0 + mod[:, d:2 * d]) + mod[:, :d]).astype(BF16)

    def proj(w):
        return jnp.dot(h, w[...], preferred_element_type=F32)

    o_aqkv[...] = proj(w_aqkv)
    o_az[...] = proj(w_az)
    o_ba[...] = proj(w_ba)
    cosa, sinm, sinp = cosa_ref[...], sinm_ref[...], sinp_ref[...]
    o_bq[...] = (_rot_axial(proj(w_bq), cosa, sinm, sinp) * (B_HD ** -0.5)).astype(BF16)
    bkv = proj(w_bkv)
    half = bkv.shape[-1] // 2
    o_bk[...] = _rot_axial(bkv[:, :half], cosa, sinm, sinp).astype(BF16)
    o_bv[...] = bkv[:, half:].astype(BF16)
    o_bz[...] = proj(w_bz)
    cqkv = proj(w_cqkv)
    cw = cqkv.shape[-1] // 3
    cosr, sinr = cosr_ref[...], sinr_ref[...]
    o_cq[...] = _rot_half128(cqkv[:, :cw], cosr, sinr).astype(BF16)
    o_ck[...] = (_rot_half128(cqkv[:, cw:2 * cw], cosr, sinr) * (C_HD ** -0.5)).astype(BF16)
    o_cv[...] = cqkv[:, 2 * cw:].astype(BF16)
    o_cz[...] = proj(w_cz)
    o_mg[...] = proj(w_mg)


INPROJ_TM = 256


def in_proj(x2d, mod, mod_row_of_tile, nw, tables, weights, seq_len):
    r, d = x2d.shape
    tm = INPROJ_TM
    tiles_per_seq = seq_len // tm
    row = lambda i: (i, 0)
    tab = lambda i: (i % tiles_per_seq, 0)
    in_specs = [pl.BlockSpec((tm, d), row),
                pl.BlockSpec((None, 1, 3 * d), lambda i: (mod_row_of_tile(i), 0, 0)),
                _const_spec((1, d))]
    in_specs += [pl.BlockSpec((tm, LANES), tab) for _ in range(5)]
    in_specs += [_const_spec(w.shape) for w in weights]
    widths = [(w.shape[1], F32) for w in weights]
    (aqkv, az, ba, bq, bkv, bz, cqkv, cz, mg) = [w.shape[1] for w in weights]
    outs = [(aqkv, F32), (az, F32), (ba, F32), (bq, BF16), (bkv // 2, BF16), (bkv // 2, BF16), (bz, F32),
            (cqkv // 3, BF16), (cqkv // 3, BF16), (cqkv // 3, BF16), (cz, F32), (mg, F32)]
    del widths
    return pl.pallas_call(
        _inproj_kernel,
        grid=(r // tm,),
        in_specs=in_specs,
        out_specs=[pl.BlockSpec((tm, n), row) for n, _ in outs],
        out_shape=[jax.ShapeDtypeStruct((r, n), dt) for n, dt in outs],
        compiler_params=_cparams(("arbitrary",)),
        name="in_proj",
    )(x2d, mod, nw, *tables, *weights)


def _merge_kernel(ya_ref, yb_ref, yc_ref, mg_ref, x_ref, mod_ref, wbr_ref, wo_ref, fw_ref, o_ref, *, final):
    d = D_MODEL
    merged = None
    for i, y_ref in enumerate((ya_ref, yb_ref, yc_ref)):
        gate_i = jax.nn.sigmoid(mg_ref[:, i * d:(i + 1) * d])
        term = gate_i * jnp.dot(y_ref[...], wbr_ref[i], preferred_element_type=F32)
        merged = term if merged is None else merged + term
    out = jnp.dot(merged.astype(BF16), wo_ref[...], preferred_element_type=F32)
    xn = x_ref[...] + mod_ref[:, 2 * d:] * out
    if final:
        xn = xn * lax.rsqrt(jnp.mean(xn * xn, axis=-1, keepdims=True) + EPS) * fw_ref[...]
    o_ref[...] = xn


MERGE_TM = 512


def merge_out(ya, yb, yc, mg, x2d, mod, mod_row_of_tile, wbr, wo, fw, final):
    r, d = x2d.shape
    tm = MERGE_TM
    row = lambda i: (i, 0)
    bw = ya.shape[1]
    return pl.pallas_call(
        functools.partial(_merge_kernel, final=final),
        grid=(r // tm,),
        in_specs=[pl.BlockSpec((tm, bw), row), pl.BlockSpec((tm, bw), row), pl.BlockSpec((tm, bw), row),
                  pl.BlockSpec((tm, N_BRANCH * d), row), pl.BlockSpec((tm, d), row),
                  pl.BlockSpec((None, 1, 3 * d), lambda i: (mod_row_of_tile(i), 0, 0)),
                  _const_spec(wbr.shape), _const_spec(wo.shape), _const_spec((1, d))],
        out_specs=pl.BlockSpec((tm, d), row),
        out_shape=jax.ShapeDtypeStruct((r, d), F32),
        compiler_params=_cparams(("arbitrary",)),
        name="merge_out",
    )(ya, yb, yc, mg, x2d, mod, wbr, wo, fw)


def _dot_nt(a, b):
    return lax.dot_general(a, b, (((1,), (1,)), ((), ())), preferred_element_type=F32)


def _dot_tn(a, b):
    return lax.dot_general(a, b, (((0,), (0,)), ((), ())), preferred_element_type=F32)


def _retention_kernel(cd_ref, q_ref, k_ref, v_ref, z_ref, qc_ref, kc_ref, vc_ref, zc_ref, nw_ref,
                      *rest, ctx_out):
    if ctx_out:
        y_ref, yc_ref, sb_scr = rest
    else:
        (y_ref, sb_scr), yc_ref = rest, None
    h = pl.program_id(1)
    c = GROUP
    dh = q_ref.shape[-1]
    cd = cd_ref[...]
    lgv = jax.nn.log_sigmoid(cd)
    lane = lax.broadcasted_iota(jnp.int32, cd.shape, 1)
    lgf = jnp.sum(jnp.where(lane == h, lgv, 0.0), axis=1, keepdims=True)
    lgb = jnp.sum(jnp.where(lane == h + C_HEADS, lgv, 0.0), axis=1, keepdims=True)
    dij = (lax.broadcasted_iota(jnp.int32, (c, c), 0) - lax.broadcasted_iota(jnp.int32, (c, c), 1)).astype(F32)
    dmask = (jnp.where(dij >= 0, jnp.exp(jnp.maximum(dij, 0.0) * lgf), 0.0)
             + jnp.where(dij <= 0, jnp.exp(jnp.maximum(-dij, 0.0) * lgb), 0.0))
    rr = lax.broadcasted_iota(jnp.int32, (c, dh), 0).astype(F32)
    qdf = jnp.exp((rr + 1.0) * lgf)
    qdb = jnp.exp((c - rr) * lgb)
    kdf = jnp.exp((c - 1.0 - rr) * lgf)
    kdb = jnp.exp(rr * lgb)
    cdf = jnp.exp(c * lgf)
    cdb = jnp.exp(c * lgb)
    nw = nw_ref[...]

    def bwd_sweep(kr, vr, n, s0):
        def body(t, s):
            g = n - 1 - t
            sb_scr[g] = s
            r0 = pl.multiple_of(g * c, c)
            kd = (kr[pl.ds(r0, c), :].astype(F32) * kdb).astype(BF16)
            return s * cdb + _dot_tn(kd, vr[pl.ds(r0, c), :])
        return lax.fori_loop(0, n, body, s0)

    def fwd_sweep(qr, kr, vr, zr, yr, n, s0):
        def body(g, s):
            r0 = pl.multiple_of(g * c, c)
            kg = kr[pl.ds(r0, c), :]
            vg = vr[pl.ds(r0, c), :]
            if yr is not None:
                qg = qr[pl.ds(r0, c), :]
                a = (_dot_nt(qg, kg) * dmask).astype(BF16)
                qf = qg.astype(F32)
                lhs = jnp.concatenate([(qf * qdf).astype(BF16), (qf * qdb).astype(BF16)], axis=1)
                rhs = jnp.concatenate([s.astype(BF16), sb_scr[g].astype(BF16)], axis=0)
                o = (jnp.dot(a, vg, preferred_element_type=F32)
                     + jnp.dot(lhs, rhs, preferred_element_type=F32))
                mu = jnp.mean(o, axis=-1, keepdims=True)
                oc = o - mu
                var = jnp.mean(oc * oc, axis=-1, keepdims=True)
                zg = zr[pl.ds(r0, c), :]
                yr[pl.ds(r0, c), :] = (oc * lax.rsqrt(var + EPS) * nw * (zg * jax.nn.sigmoid(zg))).astype(yr.dtype)
            kd = (kg.astype(F32) * kdf).astype(BF16)
            return s * cdf + _dot_tn(kd, vg)
        return lax.fori_loop(0, n, body, s0)

    zeros = jnp.zeros((dh, dh), F32)
    n_c = qc_ref.shape[0] // c
    n_x = q_ref.shape[0] // c
    s_cb = bwd_sweep(kc_ref, vc_ref, n_c, zeros)
    s_cf = fwd_sweep(qc_ref, kc_ref, vc_ref, zc_ref, yc_ref, n_c, zeros)
    bwd_sweep(k_ref, v_ref, n_x, s_cb)
    fwd_sweep(q_ref, k_ref, v_ref, z_ref, y_ref, n_x, s_cf)


def retention(cq, ck, cv, cz, cqc, ckc, cvc, czc, c_decay_l, c_norm_w_l, ctx_out):
    b, t, w = cq.shape
    l = cqc.shape[1]
    assert t % GROUP == 0 and l % GROUP == 0
    dh = C_HD
    seq = lambda n: pl.BlockSpec((None, n, dh), lambda i, j: (i, 0, j))
    out_shape = [jax.ShapeDtypeStruct((b, t, w), BF16)]
    out_specs = [seq(t)]
    if ctx_out:
        out_shape.append(jax.ShapeDtypeStruct((b, l, w), BF16))
        out_specs.append(seq(l))
    res = pl.pallas_call(
        functools.partial(_retention_kernel, ctx_out=ctx_out),
        grid=(b, C_HEADS),
        in_specs=[pl.BlockSpec((1, 2 * C_HEADS), lambda i, j: (0, 0)),
                  seq(t), seq(t), seq(t), seq(t), seq(l), seq(l), seq(l), seq(l),
                  pl.BlockSpec((1, dh), lambda i, j: (0, j))],
        out_specs=out_specs,
        out_shape=out_shape,
        scratch_shapes=[pltpu.VMEM((max(t, l) // GROUP, dh, dh), F32)],
        compiler_params=_cparams(("arbitrary", "arbitrary")),
        name="retention",
    )(c_decay_l.reshape(1, -1), cq, ck, cv, cz, cqc, ckc, cvc, czc, c_norm_w_l.reshape(1, -1))
    return res if ctx_out else (res[0], None)


def _softmax_pv(s, sink, vals):
    m = jnp.maximum(jnp.max(s, axis=-1, keepdims=True), sink)
    p = jnp.exp(s - m)
    den = jnp.sum(p, axis=-1, keepdims=True) + jnp.exp(sink - m)
    return jnp.dot(p.astype(BF16), vals, preferred_element_type=F32) * pl.reciprocal(den)


def _wattn_kernel(sink_ref, q_ref, kp_ref, kc_ref, kn_ref, vp_ref, vc_ref, vn_ref, kx_ref, vx_ref, z_ref, y_ref):
    n = pl.program_id(1)
    nb = pl.num_programs(1)
    blk = B_BLOCK
    l = kx_ref.shape[0]
    grp = B_Q_HEADS // B_KV_HEADS
    q = q_ref[...]
    lo = lax.broadcasted_iota(jnp.int32, (blk, LANES), 1) < B_HD
    ri = lax.broadcasted_iota(jnp.int32, (blk, blk), 0)
    ci = lax.broadcasted_iota(jnp.int32, (blk, blk), 1)
    ok_prev = jnp.logical_and(ci >= ri, n > 0)
    ok_next = jnp.logical_and(ci <= ri, n < nb - 1)
    zero = jnp.zeros((blk, LANES), q.dtype)
    tiles = []
    for g in range(B_KV_HEADS):
        gs = slice(g * LANES, (g + 1) * LANES)
        keys = jnp.concatenate([kx_ref[:, gs], kp_ref[:, gs], kc_ref[:, gs], kn_ref[:, gs]], axis=0)
        vals = jnp.concatenate([vx_ref[:, gs], vp_ref[:, gs], vc_ref[:, gs], vn_ref[:, gs]], axis=0)
        rows = []
        for j in range(grp):
            t = (g * grp + j) // 2
            tile = q[:, t * LANES:(t + 1) * LANES]
            rows.append(jnp.where(lo, tile, zero) if j % 2 == 0 else jnp.where(lo, zero, tile))
        s_all = _dot_nt(jnp.concatenate(rows, axis=0), keys)
        outs = []
        for j in range(grp):
            s = s_all[j * blk:(j + 1) * blk]
            s = jnp.concatenate([s[:, :l],
                                 jnp.where(ok_prev, s[:, l:l + blk], -jnp.inf),
                                 s[:, l + blk:l + 2 * blk],
                                 jnp.where(ok_next, s[:, l + 2 * blk:], -jnp.inf)], axis=1)
            sink = jnp.full((blk, 1), sink_ref[0, g * grp + j], F32)
            outs.append(_softmax_pv(s, sink, vals))
        for j in range(0, grp, 2):
            tiles.append(jnp.where(lo, outs[j], outs[j + 1]))
    z = z_ref[...]
    y_ref[...] = (jnp.concatenate(tiles, axis=1) * (z * jax.nn.sigmoid(z))).astype(y_ref.dtype)


def window_attn(bq, bk2, bv2, bz, bk2c, bv2c, sink_l):
    b, t, w = bq.shape
    l = bk2c.shape[1]
    blk = B_BLOCK
    nb = t // blk
    kw = bk2.shape[2]
    prev = pl.BlockSpec((None, blk, kw), lambda i, n: (i, jnp.maximum(n - 1, 0), 0))
    cur = pl.BlockSpec((None, blk, kw), lambda i, n: (i, n, 0))
    nxt = pl.BlockSpec((None, blk, kw), lambda i, n: (i, jnp.minimum(n + 1, nb - 1), 0))
    ctx = pl.BlockSpec((None, l, kw), lambda i, n: (i, 0, 0))
    qz = pl.BlockSpec((None, blk, w), lambda i, n: (i, n, 0))
    return pl.pallas_call(
        _wattn_kernel,
        grid=(b, nb),
        in_specs=[pl.BlockSpec(memory_space=pltpu.SMEM), qz, prev, cur, nxt, prev, cur, nxt, ctx, ctx, qz],
        out_specs=qz,
        out_shape=jax.ShapeDtypeStruct((b, t, w), BF16),
        compiler_params=_cparams(("arbitrary", "arbitrary")),
        name="window_attn",
    )(sink_l.reshape(1, -1), bq, bk2, bk2, bk2, bv2, bv2, bv2, bk2c, bv2c, bz)


def _cattn_kernel(sink_ref, q_ref, kx_ref, vx_ref, z_ref, y_ref):
    l = q_ref.shape[0]
    grp = B_Q_HEADS // B_KV_HEADS
    q = q_ref[...]
    lo = lax.broadcasted_iota(jnp.int32, (l, LANES), 1) < B_HD
    zero = jnp.zeros((l, LANES), q.dtype)
    tiles = []
    for g in range(B_KV_HEADS):
        gs = slice(g * LANES, (g + 1) * LANES)
        keys, vals = kx_ref[:, gs], vx_ref[:, gs]
        outs = []
        for j in range(grp):
            t = (g * grp + j) // 2
            tile = q[:, t * LANES:(t + 1) * LANES]
            lhs = jnp.where(lo, tile, zero) if j % 2 == 0 else jnp.where(lo, zero, tile)
            sink = jnp.full((l, 1), sink_ref[0, g * grp + j], F32)
            outs.append(_softmax_pv(_dot_nt(lhs, keys), sink, vals))
        for j in range(0, grp, 2):
            tiles.append(jnp.where(lo, outs[j], outs[j + 1]))
    z = z_ref[...]
    y_ref[...] = (jnp.concatenate(tiles, axis=1) * (z * jax.nn.sigmoid(z))).astype(y_ref.dtype)


def context_attn(bqc, bk2c, bv2c, bzc, sink_l):
    b, l, w = bqc.shape
    kw = bk2c.shape[2]
    kv = pl.BlockSpec((None, l, kw), lambda i: (i, 0, 0))
    qz = pl.BlockSpec((None, l, w), lambda i: (i, 0, 0))
    return pl.pallas_call(
        _cattn_kernel,
        grid=(b,),
        in_specs=[pl.BlockSpec(memory_space=pltpu.SMEM), qz, kv, kv, qz],
        out_specs=qz,
        out_shape=jax.ShapeDtypeStruct((b, l, w), BF16),
        compiler_params=_cparams(("arbitrary",)),
        name="context_attn",
    )(sink_l.reshape(1, -1), bqc, bk2c, bv2c, bzc)


CONV_PAD = 8
_R_GCF, _R_BF, _R_GCB, _R_BB, _R_TEF, _R_TEB = range(6)
_M_NEGF, _M_NEGB, _M_STRICTF, _M_STRICTB, _M_EYE = range(5)


def _delta_kernel(alog_ref, dtb_ref, cwq_ref, cwk_ref, cwv_ref, nw_ref,
                  xq_ref, xk_ref, xv_ref, z_ref, bat_ref,
                  xqc_ref, xkc_ref, xvc_ref, zc_ref, batc_ref,
                  *rest, ctx_out):
    if ctx_out:
        y_ref, yc_ref = rest[:2]
        rest = rest[2:]
    else:
        y_ref, yc_ref = rest[0], None
        rest = rest[1:]
    xp, qn, kn, vn, of, ob, gates, masks = rest
    h = pl.program_id(1)
    gsz = GROUP
    ck = A_CHUNK
    nck = gsz // ck
    dk = A_DK

    ri = lax.broadcasted_iota(jnp.int32, (gsz, gsz), 0)
    ci = lax.broadcasted_iota(jnp.int32, (gsz, gsz), 1)
    same = (ri // ck) == (ci // ck)
    masks[_M_NEGF] = jnp.where(jnp.logical_and(same, ri >= ci), 0.0, -jnp.inf)
    masks[_M_NEGB] = jnp.where(jnp.logical_and(same, ri <= ci), 0.0, -jnp.inf)
    masks[_M_STRICTF] = jnp.where(jnp.logical_and(same, ri > ci), 1.0, 0.0)
    masks[_M_STRICTB] = jnp.where(jnp.logical_and(same, ri < ci), 1.0, 0.0)
    masks[_M_EYE] = jnp.where(ri == ci, 1.0, 0.0)

    lane8 = lax.broadcasted_iota(jnp.int32, alog_ref.shape, 1)
    a_all = jnp.exp(alog_ref[...])
    dt_all = dtb_ref[...]

    def pick(v, idx):
        return jnp.sum(jnp.where(lane8 == idx, v, 0.0), axis=1, keepdims=True)

    a_f, a_b = pick(a_all, h), pick(a_all, h + A_HEADS)
    dt_f, dt_b = pick(dt_all, h), pick(dt_all, h + A_HEADS)
    nw = nw_ref[...]

    def conv_silu(x_ref, cw_ref, n):
        zpad = jnp.zeros((CONV_PAD, dk), F32)
        xp[0:CONV_PAD, :] = zpad
        xp[CONV_PAD + n:2 * CONV_PAD + n, :] = zpad

        def cp(i, _):
            r0 = pl.multiple_of(i * gsz, gsz)
            xp[pl.ds(CONV_PAD + r0, gsz), :] = x_ref[pl.ds(r0, gsz), :]
            return 0
        lax.fori_loop(0, n // gsz, cp, 0)
        return cw_ref[...]

    def conv_tile(w, r0):
        acc = None
        for j in range(A_CONV):
            term = xp[pl.ds(r0 + (CONV_PAD - A_CONV // 2 + j), gsz), :] * w[j:j + 1, :]
            acc = term if acc is None else acc + term
        return acc * jax.nn.sigmoid(acc)

    def prologue(xq, xk, xv, bat, n):
        ng = n // gsz
        w = conv_silu(xq, cwq_ref, n)

        def bq(i, _):
            r0 = pl.multiple_of(i * gsz, gsz)
            s = conv_tile(w, r0)
            qn[pl.ds(r0, gsz), :] = (s * lax.rsqrt(jnp.sum(s * s, axis=-1, keepdims=True) + EPS)
                                     * (dk ** -0.5)).astype(BF16)
            return 0
        lax.fori_loop(0, ng, bq, 0)
        w = conv_silu(xk, cwk_ref, n)

        def bk(i, _):
            r0 = pl.multiple_of(i * gsz, gsz)
            s = conv_tile(w, r0)
            kn[pl.ds(r0, gsz), :] = (s * lax.rsqrt(jnp.sum(s * s, axis=-1, keepdims=True) + EPS)).astype(BF16)
            return 0
        lax.fori_loop(0, ng, bk, 0)
        w = conv_silu(xv, cwv_ref, n)

        def bv(i, _):
            r0 = pl.multiple_of(i * gsz, gsz)
            vn[pl.ds(r0, gsz), :] = conv_tile(w, r0).astype(BF16)
            return 0
        lax.fori_loop(0, ng, bv, 0)

        beta_f = jax.nn.sigmoid(bat[pl.ds(h, 1), :])
        beta_b = jax.nn.sigmoid(bat[pl.ds(h + A_HEADS, 1), :])
        g_f = -a_f * jax.nn.softplus(bat[pl.ds(h + 2 * A_HEADS, 1), :] + dt_f)
        g_b = -a_b * jax.nn.softplus(bat[pl.ds(h + 3 * A_HEADS, 1), :] + dt_b)
        row = lax.broadcasted_iota(jnp.int32, (8, n), 0)
        pos = lax.broadcasted_iota(jnp.int32, (8, n), 1) % ck
        g2 = jnp.where(row == 0, g_f, jnp.where(row == 1, g_b, 0.0))
        cf, cb = g2, g2
        s = 1
        while s < ck:
            cf = cf + jnp.where(pos >= s, pltpu.roll(cf, s, 1), 0.0)
            cb = cb + jnp.where(pos < ck - s, pltpu.roll(cb, n - s, 1), 0.0)
            s *= 2
        tab = jnp.where(row == _R_GCF, cf[0:1], 0.0)
        tab = jnp.where(row == _R_BF, beta_f, tab)
        tab = jnp.where(row == _R_GCB, cb[1:2], tab)
        tab = jnp.where(row == _R_BB, beta_b, tab)
        tab = jnp.where(row == _R_TEF, cb[0:1] - g_f, tab)
        tab = jnp.where(row == _R_TEB, cf[1:2] - g_b, tab)
        for g in range(ng):
            gates[g] = tab[:, g * gsz:(g + 1) * gsz]

    def group_step(g, s, fwd, oacc):
        r0 = pl.multiple_of(g * gsz, gsz)
        kg = kn[pl.ds(r0, gsz), :]
        qg = qn[pl.ds(r0, gsz), :]
        vg = vn[pl.ds(r0, gsz), :]
        rg = gates[g]
        rgt = rg.T
        i_gc, i_b, i_te = (_R_GCF, _R_BF, _R_TEF) if fwd else (_R_GCB, _R_BB, _R_TEB)
        gc_row, b_row, te_row = rg[i_gc:i_gc + 1], rg[i_b:i_b + 1], rg[i_te:i_te + 1]
        gc_col, b_col, te_col = rgt[:, i_gc:i_gc + 1], rgt[:, i_b:i_b + 1], rgt[:, i_te:i_te + 1]
        dec = jnp.exp((gc_col - gc_row) + masks[_M_NEGF if fwd else _M_NEGB])
        ab = _dot_nt(jnp.concatenate([kg, qg], axis=0), kg)
        lm = ab[:gsz] * b_col * dec * masks[_M_STRICTF if fwd else _M_STRICTB]
        qm = (ab[gsz:] * dec).astype(BF16)
        x = masks[_M_EYE] - lm
        pb = lm.astype(BF16)
        p = jnp.dot(pb, pb, preferred_element_type=F32)
        order = 2
        while order < ck:
            pb = p.astype(BF16)
            if 2 * order < ck:
                xp_pp = jnp.dot(jnp.concatenate([x.astype(BF16), pb], axis=0), pb, preferred_element_type=F32)
                x = x + xp_pp[:gsz]
                p = xp_pp[gsz:]
            else:
                x = x + jnp.dot(x.astype(BF16), pb, preferred_element_type=F32)
            order *= 2
        tb = x * b_row
        tbg = tb * jnp.exp(gc_row)
        uw = jnp.dot(jnp.concatenate([tb.astype(BF16), tbg.astype(BF16)], axis=0),
                     jnp.concatenate([vg, kg], axis=1), preferred_element_type=F32)
        u = uw[:gsz, :dk]
        w = uw[gsz:, dk:].astype(BF16)
        qd = (qg.astype(F32) * jnp.exp(gc_col)).astype(BF16)
        kt = (kg.astype(F32) * jnp.exp(te_col)).astype(BF16)
        tot = jnp.exp(gc_row + te_row)
        for c in (range(nck) if fwd else range(nck - 1, -1, -1)):
            rows = slice(c * ck, (c + 1) * ck)
            ws = jnp.dot(jnp.concatenate([w[rows], qd[rows]], axis=0), s.astype(BF16), preferred_element_type=F32)
            vnew = (u[rows] - ws[:ck]).astype(BF16)
            oacc[pl.ds(r0 + c * ck, ck), :] = ws[ck:] + jnp.dot(qm[rows, rows], vnew, preferred_element_type=F32)
            s = s * tot[:, c * ck:c * ck + 1] + _dot_tn(kt[rows], vnew)
        return s

    def sweeps(n, s_f, s_b):
        ng = n // gsz

        def body(t, carry):
            return (group_step(t, carry[0], True, of), group_step(ng - 1 - t, carry[1], False, ob))
        return lax.fori_loop(0, ng, body, (s_f, s_b))

    def epilogue(zr, yr, n):
        def body(i, _):
            r0 = pl.multiple_of(i * gsz, gsz)
            o = of[pl.ds(r0, gsz), :] + ob[pl.ds(r0, gsz), :]
            zg = zr[pl.ds(r0, gsz), :]
            yr[pl.ds(r0, gsz), :] = (o * lax.rsqrt(jnp.mean(o * o, axis=-1, keepdims=True) + EPS) * nw
                                     * (zg * jax.nn.sigmoid(zg))).astype(yr.dtype)
            return 0
        lax.fori_loop(0, n // gsz, body, 0)

    n_c, n_x = xqc_ref.shape[0], xq_ref.shape[0]
    zeros = jnp.zeros((dk, dk), F32)
    prologue(xqc_ref, xkc_ref, xvc_ref, batc_ref, n_c)
    s_f, s_b = sweeps(n_c, zeros, zeros)
    if ctx_out:
        epilogue(zc_ref, yc_ref, n_c)
    prologue(xq_ref, xk_ref, xv_ref, bat_ref, n_x)
    sweeps(n_x, s_f, s_b)
    epilogue(z_ref, y_ref, n_x)


def delta_mixer(aqkv, az, bat, aqkvc, azc, batc, conv_w_l, a_log_l, dt_bias_l, norm_w_l, ctx_out):
    b, t, _ = aqkv.shape
    l = aqkvc.shape[1]
    assert t % GROUP == 0 and l % GROUP == 0
    dk = A_DK
    nh = A_HEADS
    col = lambda n, off: pl.BlockSpec((None, n, dk), lambda i, j: (i, 0, j + off))
    cw = lambda off: pl.BlockSpec((A_CONV, dk), lambda i, j: (0, j + off))
    small = pl.BlockSpec((1, 2 * nh), lambda i, j: (0, 0))
    gate = lambda n: pl.BlockSpec((None, 4 * nh, n), lambda i, j: (i, 0, 0))
    out_shape = [jax.ShapeDtypeStruct((b, t, nh * dk), BF16)]
    out_specs = [col(t, 0)]
    if ctx_out:
        out_shape.append(jax.ShapeDtypeStruct((b, l, nh * dk), BF16))
        out_specs.append(col(l, 0))
    nmax = max(t, l)
    res = pl.pallas_call(
        functools.partial(_delta_kernel, ctx_out=ctx_out),
        grid=(b, nh),
        in_specs=[small, small, cw(0), cw(nh), cw(2 * nh), pl.BlockSpec((1, dk), lambda i, j: (0, 0)),
                  col(t, 0), col(t, nh), col(t, 2 * nh), col(t, 0), gate(t),
                  col(l, 0), col(l, nh), col(l, 2 * nh), col(l, 0), gate(l)],
        out_specs=out_specs,
        out_shape=out_shape,
        scratch_shapes=[pltpu.VMEM((nmax + 2 * CONV_PAD, dk), F32),
                        pltpu.VMEM((nmax, dk), BF16), pltpu.VMEM((nmax, dk), BF16), pltpu.VMEM((nmax, dk), BF16),
                        pltpu.VMEM((nmax, dk), F32), pltpu.VMEM((nmax, dk), F32),
                        pltpu.VMEM((nmax // GROUP, 8, GROUP), F32),
                        pltpu.VMEM((5, GROUP, GROUP), F32)],
        compiler_params=_cparams(("arbitrary", "arbitrary")),
        name="delta_mixer",
    )(a_log_l.reshape(1, -1), dt_bias_l.reshape(1, -1), conv_w_l, conv_w_l, conv_w_l, norm_w_l.reshape(1, -1),
      aqkv, aqkv, aqkv, az, bat, aqkvc, aqkvc, aqkvc, azc, batc)
    return res if ctx_out else (res[0], None)


def _rope_angles(pos, n_freq):
    inv = ROPE_BASE ** (-jnp.arange(n_freq, dtype=F32) / n_freq)
    return pos[:, None] * inv[None, :]


def _position_tables(t):
    rows_n = t // GRID_W
    rows = jnp.repeat(jnp.arange(rows_n, dtype=F32), GRID_W)
    cols = jnp.tile(jnp.arange(GRID_W, dtype=F32), rows_n)
    n_ax = B_HD // 4
    ar, ac = _rope_angles(rows, n_ax), _rope_angles(cols, n_ax)
    cr, sr, cc, sc = jnp.cos(ar), jnp.sin(ar), jnp.cos(ac), jnp.sin(ac)
    zz = jnp.zeros_like(sr)
    reps = LANES // B_HD
    cosa = jnp.tile(jnp.concatenate([cr, cr, cc, cc], axis=1), (1, reps))
    sinm = jnp.tile(jnp.concatenate([-sr, zz, -sc, zz], axis=1), (1, reps))
    sinp = jnp.tile(jnp.concatenate([zz, sr, zz, sc], axis=1), (1, reps))
    at = _rope_angles(jnp.arange(t, dtype=F32), C_HD // 2)
    cosr = jnp.concatenate([jnp.cos(at), jnp.cos(at)], axis=1)
    sinr = jnp.concatenate([-jnp.sin(at), jnp.sin(at)], axis=1)
    return (cosa, sinm, sinp, cosr, sinr)


def _identity_tables(l):
    one, zero = jnp.ones((l, LANES), F32), jnp.zeros((l, LANES), F32)
    return (one, zero, zero, one, zero)


def _split_weights(w):
    sizes = (3 * A_WIDTH, A_WIDTH, 2 * A_HEADS, 2 * A_HEADS, B_Q_HEADS * B_HD, 2 * B_KV_HEADS * B_HD, BR_WIDTH,
             3 * BR_WIDTH, BR_WIDTH, N_BRANCH * D_MODEL)
    pts = np.cumsum(sizes)[:-1]
    aqkv, az, abeta, aalpha, bq, bkv, bz, cqkv, cz, mg = jnp.split(w, pts, axis=1)
    ba = jnp.concatenate([abeta, aalpha], axis=1)
    ba = jnp.pad(ba, ((0, 0), (0, LANES - ba.shape[1])))
    heads = [bkv[:, i * B_HD:(i + 1) * B_HD] for i in range(2 * B_KV_HEADS)]
    bkv2 = jnp.concatenate([hd for hd in heads for _ in range(LANES // B_HD)], axis=1)
    return [a.astype(BF16) for a in (aqkv, az, ba, bq, bkv2, bz, cqkv, cz, mg)]


def kernel(x, c, ctx, c_ctx, w_ada, b_ada, norm_w, w_in, a_conv_w, a_log, a_dt_bias, a_norm_w, b_sink, c_decay,
           c_norm_w, w_branch, w_out, final_norm_w):
    b, t, d = x.shape
    l = ctx.shape[1]
    depth = w_ada.shape[0]
    assert d == D_MODEL and t % INPROJ_TM == 0 and l % INPROJ_TM == 0
    assert t % MERGE_TM == 0 and (b * l) % MERGE_TM == 0 and t % GRID_W == 0

    n_mod = -(-(b + 1) // 8) * 8
    cc = jnp.concatenate([c, c_ctx[None, :], jnp.zeros((n_mod - b - 1, d), F32)], axis=0)
    mod_all = ada_mod(cc, w_ada, b_ada)
    x_tables = _position_tables(t)
    c_tables = _identity_tables(l)

    x2d = x.reshape(b * t, d)
    c2d = ctx.reshape(b * l, d)
    for layer in range(depth):
        last = layer == depth - 1
        mod = mod_all[layer].reshape(n_mod, 1, 3 * d)
        weights = _split_weights(w_in[layer])
        nw = norm_w[layer].reshape(1, d)
        px = in_proj(x2d, mod, lambda i: i // (t // INPROJ_TM), nw, x_tables, weights, t)
        pc = in_proj(c2d, mod, lambda i: b, nw, c_tables, weights, l)
        (aqkv, az, ba, bq, bk, bv, bz, cq, ck, cv, cz, mg) = [a.reshape(b, t, -1) for a in px]
        (aqkvc, azc, bac, bqc, bkc, bvc, bzc, cqc, ckc, cvc, czc, mgc) = [a.reshape(b, l, -1) for a in pc]
        bat = jnp.swapaxes(ba[:, :, :4 * A_HEADS], 1, 2)
        batc = jnp.swapaxes(bac[:, :, :4 * A_HEADS], 1, 2)

        ya, yac = delta_mixer(aqkv, az, bat, aqkvc, azc, batc, a_conv_w[layer], a_log[layer], a_dt_bias[layer],
                              a_norm_w[layer], not last)
        yb = window_attn(bq, bk, bv, bz, bkc, bvc, b_sink[layer])
        yc, ycc = retention(cq, ck, cv, cz, cqc, ckc, cvc, czc, c_decay[layer], c_norm_w[layer], not last)

        wbr = w_branch[layer].astype(BF16)
        wo = w_out[layer].astype(BF16)
        fw = final_norm_w.reshape(1, d)
        flat = lambda a: a.reshape(-1, a.shape[-1])
        x2d_new = merge_out(flat(ya), flat(yb), flat(yc), flat(mg), x2d, mod, lambda i: i // (t // MERGE_TM),
                            wbr, wo, fw, last)
        if not last:
            ybc = context_attn(bqc, bkc, bvc, bzc, b_sink[layer])
            c2d = merge_out(flat(yac), flat(ybc), flat(ycc), flat(mgc), c2d, mod, lambda i: b, wbr, wo, fw, False)
        x2d = x2d_new
    return x2d.reshape(b, t, d)
```

```python
import functools

import jax
import jax.numpy as jnp
import numpy as np
from jax import lax
from jax.experimental import pallas as pl
from jax.experimental.pallas import tpu as pltpu

F32 = jnp.float32
BF16 = jnp.bfloat16

D_MODEL = 1024
GRID_W = 64
EPS = 1e-6
ROPE_BASE = 10000.0
BR_WIDTH = D_MODEL // 2
A_DK = 128
A_HEADS = BR_WIDTH // A_DK
A_WIDTH = A_HEADS * A_DK
A_CONV = 5
A_CHUNK = 64
B_HD = 64
B_Q_HEADS = BR_WIDTH // B_HD
B_KV_HEADS = B_Q_HEADS // 4
B_BLOCK = 128
C_HD = 128
C_HEADS = BR_WIDTH // C_HD
N_BRANCH = 3

LANES = 128
GROUP = 256
VMEM_LIMIT = 56 * 1024 * 1024


def _cparams(sem):
    return pltpu.CompilerParams(dimension_semantics=sem, vmem_limit_bytes=VMEM_LIMIT)


def _const_spec(shape):
    nd = len(shape)
    return pl.BlockSpec(shape, lambda *_: (0,) * nd, pipeline_mode=pl.Buffered(1))


def _ada_kernel(c_ref, w_ref, b_ref, o_ref):
    c = c_ref[...]
    s = c * jax.nn.sigmoid(c)
    o_ref[...] = jnp.dot(s, w_ref[...], preferred_element_type=F32,
                         precision=lax.Precision.HIGHEST) + b_ref[...]


def ada_mod(cc, w_ada, b_ada):
    depth = w_ada.shape[0]
    r, d = cc.shape
    tn = 1024
    nt = (3 * d) // tn
    return pl.pallas_call(
        _ada_kernel,
        grid=(depth, nt),
        in_specs=[pl.BlockSpec((r, d), lambda l, j: (0, 0)),
                  pl.BlockSpec((None, d, tn), lambda l, j: (l, 0, j)),
                  pl.BlockSpec((None, 1, tn), lambda l, j: (l, 0, j))],
        out_specs=pl.BlockSpec((None, r, tn), lambda l, j: (l, 0, j)),
        out_shape=jax.ShapeDtypeStruct((depth, r, 3 * d), F32),
        compiler_params=_cparams(("arbitrary", "arbitrary")),
        name="ada_mod",
    )(cc, w_ada, b_ada.reshape(depth, 1, 3 * d))


def _rot_axial(x, cosa, sinm, sinp):
    w = x.shape[-1]
    reps = w // LANES
    ca = jnp.concatenate([cosa] * reps, axis=-1)
    sm = jnp.concatenate([sinm] * reps, axis=-1)
    sp = jnp.concatenate([sinp] * reps, axis=-1)
    return x * ca + pltpu.roll(x, w - 16, 1) * sm + pltpu.roll(x, 16, 1) * sp


def _rot_half128(x, cosr, sinr):
    outs = []
    for t in range(x.shape[-1] // LANES):
        xt = x[:, t * LANES:(t + 1) * LANES]
        outs.append(xt * cosr + pltpu.roll(xt, LANES // 2, 1) * sinr)
    return jnp.concatenate(outs, axis=-1)


def _inproj_kernel(x_ref, mod_ref, nw_ref, cosa_ref, sinm_ref, sinp_ref, cosr_ref, sinr_ref,
                   w_aqkv, w_az, w_ba, w_bq, w_bkv, w_bz, w_cqkv, w_cz, w_mg,
                   o_aqkv, o_az, o_ba, o_bq, o_bk, o_bv, o_bz, o_cq, o_ck, o_cv, o_cz, o_mg):
    d = D_MODEL
    x = x_ref[...]
    y = x * lax.rsqrt(jnp.mean(x * x, axis=-1, keepdims=True) + EPS) * nw_ref[...]
    mod = mod_ref[...]
    h = (y * (1.0 + mod[:, d:2 * d]) + mod[:, :d]).astype(BF16)

    def proj(w):
        return jnp.dot(h, w[...], preferred_element_type=F32)

    o_aqkv[...] = proj(w_aqkv)
    o_az[...] = proj(w_az)
    o_ba[...] = proj(w_ba)
    cosa, sinm, sinp = cosa_ref[...], sinm_ref[...], sinp_ref[...]
    o_bq[...] = (_rot_axial(proj(w_bq), cosa, sinm, sinp) * (B_HD ** -0.5)).astype(BF16)
    bkv = proj(w_bkv)
    half = bkv.shape[-1] // 2
    o_bk[...] = _rot_axial(bkv[:, :half], cosa, sinm, sinp).astype(BF16)
    o_bv[...] = bkv[:, half:].astype(BF16)
    o_bz[...] = proj(w_bz)
    cqkv = proj(w_cqkv)
    cw = cqkv.shape[-1] // 3
    cosr, sinr = cosr_ref[...], sinr_ref[...]
    o_cq[...] = _rot_half128(cqkv[:, :cw], cosr, sinr).astype(BF16)
    o_ck[...] = (_rot_half128(cqkv[:, cw:2 * cw], cosr, sinr) * (C_HD ** -0.5)).astype(BF16)
    o_cv[...] = cqkv[:, 2 * cw:].astype(BF16)
    o_cz[...] = proj(w_cz)
    o_mg[...] = proj(w_mg)


INPROJ_TM = 256


def in_proj(x2d, mod, mod_row_of_tile, nw, tables, weights, seq_len):
    r, d = x2d.shape
    tm = INPROJ_TM
    tiles_per_seq = seq_len // tm
    row = lambda i: (i, 0)
    tab = lambda i: (i % tiles_per_seq, 0)
    in_specs = [pl.BlockSpec((tm, d), row),
                pl.BlockSpec((None, 1, 3 * d), lambda i: (mod_row_of_tile(i), 0, 0)),
                _const_spec((1, d))]
    in_specs += [pl.BlockSpec((tm, LANES), tab) for _ in range(5)]
    in_specs += [_const_spec(w.shape) for w in weights]
    widths = [(w.shape[1], F32) for w in weights]
    (aqkv, az, ba, bq, bkv, bz, cqkv, cz, mg) = [w.shape[1] for w in weights]
    outs = [(aqkv, F32), (az, F32), (ba, F32), (bq, BF16), (bkv // 2, BF16), (bkv // 2, BF16), (bz, F32),
            (cqkv // 3, BF16), (cqkv // 3, BF16), (cqkv // 3, BF16), (cz, F32), (mg, F32)]
    del widths
    return pl.pallas_call(
        _inproj_kernel,
        grid=(r // tm,),
        in_specs=in_specs,
        out_specs=[pl.BlockSpec((tm, n), row) for n, _ in outs],
        out_shape=[jax.ShapeDtypeStruct((r, n), dt) for n, dt in outs],
        compiler_params=_cparams(("arbitrary",)),
        name="in_proj",
    )(x2d, mod, nw, *tables, *weights)


def _merge_kernel(ya_ref, yb_ref, yc_ref, mg_ref, x_ref, mod_ref, wbr_ref, wo_ref, fw_ref, o_ref, *, final):
    d = D_MODEL
    merged = None
    for i, y_ref in enumerate((ya_ref, yb_ref, yc_ref)):
        gate_i = jax.nn.sigmoid(mg_ref[:, i * d:(i + 1) * d])
        term = gate_i * jnp.dot(y_ref[...], wbr_ref[i], preferred_element_type=F32)
        merged = term if merged is None else merged + term
    out = jnp.dot(merged.astype(BF16), wo_ref[...], preferred_element_type=F32)
    xn = x_ref[...] + mod_ref[:, 2 * d:] * out
    if final:
        xn = xn * lax.rsqrt(jnp.mean(xn * xn, axis=-1, keepdims=True) + EPS) * fw_ref[...]
    o_ref[...] = xn


MERGE_TM = 512


def merge_out(ya, yb, yc, mg, x2d, mod, mod_row_of_tile, wbr, wo, fw, final):
    r, d = x2d.shape
    tm = MERGE_TM
    row = lambda i: (i, 0)
    bw = ya.shape[1]
    return pl.pallas_call(
        functools.partial(_merge_kernel, final=final),
        grid=(r // tm,),
        in_specs=[pl.BlockSpec((tm, bw), row), pl.BlockSpec((tm, bw), row), pl.BlockSpec((tm, bw), row),
                  pl.BlockSpec((tm, N_BRANCH * d), row), pl.BlockSpec((tm, d), row),
                  pl.BlockSpec((None, 1, 3 * d), lambda i: (mod_row_of_tile(i), 0, 0)),
                  _const_spec(wbr.shape), _const_spec(wo.shape), _const_spec((1, d))],
        out_specs=pl.BlockSpec((tm, d), row),
        out_shape=jax.ShapeDtypeStruct((r, d), F32),
        compiler_params=_cparams(("arbitrary",)),
        name="merge_out",
    )(ya, yb, yc, mg, x2d, mod, wbr, wo, fw)


def _dot_nt(a, b):
    return lax.dot_general(a, b, (((1,), (1,)), ((), ())), preferred_element_type=F32)


def _dot_tn(a, b):
    return lax.dot_general(a, b, (((0,), (0,)), ((), ())), preferred_element_type=F32)


def _retention_kernel(cd_ref, q_ref, k_ref, v_ref, z_ref, qc_ref, kc_ref, vc_ref, zc_ref, nw_ref,
                      *rest, ctx_out):
    if ctx_out:
        y_ref, yc_ref, sb_scr = rest
    else:
        (y_ref, sb_scr), yc_ref = rest, None
    h = pl.program_id(1)
    c = GROUP
    dh = q_ref.shape[-1]
    cd = cd_ref[...]
    lgv = jax.nn.log_sigmoid(cd)
    lane = lax.broadcasted_iota(jnp.int32, cd.shape, 1)
    lgf = jnp.sum(jnp.where(lane == h, lgv, 0.0), axis=1, keepdims=True)
    lgb = jnp.sum(jnp.where(lane == h + C_HEADS, lgv, 0.0), axis=1, keepdims=True)
    dij = (lax.broadcasted_iota(jnp.int32, (c, c), 0) - lax.broadcasted_iota(jnp.int32, (c, c), 1)).astype(F32)
    dmask = (jnp.where(dij >= 0, jnp.exp(jnp.maximum(dij, 0.0) * lgf), 0.0)
             + jnp.where(dij <= 0, jnp.exp(jnp.maximum(-dij, 0.0) * lgb), 0.0))
    rr = lax.broadcasted_iota(jnp.int32, (c, dh), 0).astype(F32)
    qdf = jnp.exp((rr + 1.0) * lgf)
    qdb = jnp.exp((c - rr) * lgb)
    kdf = jnp.exp((c - 1.0 - rr) * lgf)
    kdb = jnp.exp(rr * lgb)
    cdf = jnp.exp(c * lgf)
    cdb = jnp.exp(c * lgb)
    nw = nw_ref[...]

    def bwd_sweep(kr, vr, n, s0):
        def body(t, s):
            g = n - 1 - t
            sb_scr[g] = s
            r0 = pl.multiple_of(g * c, c)
            kd = (kr[pl.ds(r0, c), :].astype(F32) * kdb).astype(BF16)
            return s * cdb + _dot_tn(kd, vr[pl.ds(r0, c), :])
        return lax.fori_loop(0, n, body, s0)

    def fwd_sweep(qr, kr, vr, zr, yr, n, s0):
        def body(g, s):
            r0 = pl.multiple_of(g * c, c)
            kg = kr[pl.ds(r0, c), :]
            vg = vr[pl.ds(r0, c), :]
            if yr is not None:
                qg = qr[pl.ds(r0, c), :]
                a = (_dot_nt(qg, kg) * dmask).astype(BF16)
                qf = qg.astype(F32)
                lhs = jnp.concatenate([(qf * qdf).astype(BF16), (qf * qdb).astype(BF16)], axis=1)
                rhs = jnp.concatenate([s.astype(BF16), sb_scr[g].astype(BF16)], axis=0)
                o = (jnp.dot(a, vg, preferred_element_type=F32)
                     + jnp.dot(lhs, rhs, preferred_element_type=F32))
                mu = jnp.mean(o, axis=-1, keepdims=True)
                oc = o - mu
                var = jnp.mean(oc * oc, axis=-1, keepdims=True)
                zg = zr[pl.ds(r0, c), :]
                yr[pl.ds(r0, c), :] = (oc * lax.rsqrt(var + EPS) * nw * (zg * jax.nn.sigmoid(zg))).astype(yr.dtype)
            kd = (kg.astype(F32) * kdf).astype(BF16)
            return s * cdf + _dot_tn(kd, vg)
        return lax.fori_loop(0, n, body, s0)

    zeros = jnp.zeros((dh, dh), F32)
    n_c = qc_ref.shape[0] // c
    n_x = q_ref.shape[0] // c
    s_cb = bwd_sweep(kc_ref, vc_ref, n_c, zeros)
    s_cf = fwd_sweep(qc_ref, kc_ref, vc_ref, zc_ref, yc_ref, n_c, zeros)
    bwd_sweep(k_ref, v_ref, n_x, s_cb)
    fwd_sweep(q_ref, k_ref, v_ref, z_ref, y_ref, n_x, s_cf)


def retention(cq, ck, cv, cz, cqc, ckc, cvc, czc, c_decay_l, c_norm_w_l, ctx_out):
    b, t, w = cq.shape
    l = cqc.shape[1]
    assert t % GROUP == 0 and l % GROUP == 0
    dh = C_HD
    seq = lambda n: pl.BlockSpec((None, n, dh), lambda i, j: (i, 0, j))
    out_shape = [jax.ShapeDtypeStruct((b, t, w), BF16)]
    out_specs = [seq(t)]
    if ctx_out:
        out_shape.append(jax.ShapeDtypeStruct((b, l, w), BF16))
        out_specs.append(seq(l))
    res = pl.pallas_call(
        functools.partial(_retention_kernel, ctx_out=ctx_out),
        grid=(b, C_HEADS),
        in_specs=[pl.BlockSpec((1, 2 * C_HEADS), lambda i, j: (0, 0)),
                  seq(t), seq(t), seq(t), seq(t), seq(l), seq(l), seq(l), seq(l),
                  pl.BlockSpec((1, dh), lambda i, j: (0, j))],
        out_specs=out_specs,
        out_shape=out_shape,
        scratch_shapes=[pltpu.VMEM((max(t, l) // GROUP, dh, dh), F32)],
        compiler_params=_cparams(("arbitrary", "arbitrary")),
        name="retention",
    )(c_decay_l.reshape(1, -1), cq, ck, cv, cz, cqc, ckc, cvc, czc, c_norm_w_l.reshape(1, -1))
    return res if ctx_out else (res[0], None)


def _softmax_pv(s, sink, vals):
    m = jnp.maximum(jnp.max(s, axis=-1, keepdims=True), sink)
    p = jnp.exp(s - m)
    den = jnp.sum(p, axis=-1, keepdims=True) + jnp.exp(sink - m)
    return jnp.dot(p.astype(BF16), vals, preferred_element_type=F32) * pl.reciprocal(den)


def _wattn_kernel(sink_ref, q_ref, kp_ref, kc_ref, kn_ref, vp_ref, vc_ref, vn_ref, kx_ref, vx_ref, z_ref, y_ref):
    n = pl.program_id(1)
    nb = pl.num_programs(1)
    blk = B_BLOCK
    l = kx_ref.shape[0]
    grp = B_Q_HEADS // B_KV_HEADS
    q = q_ref[...]
    lo = lax.broadcasted_iota(jnp.int32, (blk, LANES), 1) < B_HD
    ri = lax.broadcasted_iota(jnp.int32, (blk, blk), 0)
    ci = lax.broadcasted_iota(jnp.int32, (blk, blk), 1)
    ok_prev = jnp.logical_and(ci >= ri, n > 0)
    ok_next = jnp.logical_and(ci <= ri, n < nb - 1)
    zero = jnp.zeros((blk, LANES), q.dtype)
    tiles = []
    for g in range(B_KV_HEADS):
        gs = slice(g * LANES, (g + 1) * LANES)
        keys = jnp.concatenate([kx_ref[:, gs], kp_ref[:, gs], kc_ref[:, gs], kn_ref[:, gs]], axis=0)
        vals = jnp.concatenate([vx_ref[:, gs], vp_ref[:, gs], vc_ref[:, gs], vn_ref[:, gs]], axis=0)
        rows = []
        for j in range(grp):
            t = (g * grp + j) // 2
            tile = q[:, t * LANES:(t + 1) * LANES]
            rows.append(jnp.where(lo, tile, zero) if j % 2 == 0 else jnp.where(lo, zero, tile))
        s_all = _dot_nt(jnp.concatenate(rows, axis=0), keys)
        outs = []
        for j in range(grp):
            s = s_all[j * blk:(j + 1) * blk]
            s = jnp.concatenate([s[:, :l],
                                 jnp.where(ok_prev, s[:, l:l + blk], -jnp.inf),
                                 s[:, l + blk:l + 2 * blk],
                                 jnp.where(ok_next, s[:, l + 2 * blk:], -jnp.inf)], axis=1)
            sink = jnp.full((blk, 1), sink_ref[0, g * grp + j], F32)
            outs.append(_softmax_pv(s, sink, vals))
        for j in range(0, grp, 2):
            tiles.append(jnp.where(lo, outs[j], outs[j + 1]))
    z = z_ref[...]
    y_ref[...] = (jnp.concatenate(tiles, axis=1) * (z * jax.nn.sigmoid(z))).astype(y_ref.dtype)


def window_attn(bq, bk2, bv2, bz, bk2c, bv2c, sink_l):
    b, t, w = bq.shape
    l = bk2c.shape[1]
    blk = B_BLOCK
    nb = t // blk
    kw = bk2.shape[2]
    prev = pl.BlockSpec((None, blk, kw), lambda i, n: (i, jnp.maximum(n - 1, 0), 0))
    cur = pl.BlockSpec((None, blk, kw), lambda i, n: (i, n, 0))
    nxt = pl.BlockSpec((None, blk, kw), lambda i, n: (i, jnp.minimum(n + 1, nb - 1), 0))
    ctx = pl.BlockSpec((None, l, kw), lambda i, n: (i, 0, 0))
    qz = pl.BlockSpec((None, blk, w), lambda i, n: (i, n, 0))
    return pl.pallas_call(
        _wattn_kernel,
        grid=(b, nb),
        in_specs=[pl.BlockSpec(memory_space=pltpu.SMEM), qz, prev, cur, nxt, prev, cur, nxt, ctx, ctx, qz],
        out_specs=qz,
        out_shape=jax.ShapeDtypeStruct((b, t, w), BF16),
        compiler_params=_cparams(("arbitrary", "arbitrary")),
        name="window_attn",
    )(sink_l.reshape(1, -1), bq, bk2, bk2, bk2, bv2, bv2, bv2, bk2c, bv2c, bz)


def _cattn_kernel(sink_ref, q_ref, kx_ref, vx_ref, z_ref, y_ref):
    l = q_ref.shape[0]
    grp = B_Q_HEADS // B_KV_HEADS
    q = q_ref[...]
    lo = lax.broadcasted_iota(jnp.int32, (l, LANES), 1) < B_HD
    zero = jnp.zeros((l, LANES), q.dtype)
    tiles = []
    for g in range(B_KV_HEADS):
        gs = slice(g * LANES, (g + 1) * LANES)
        keys, vals = kx_ref[:, gs], vx_ref[:, gs]
        outs = []
        for j in range(grp):
            t = (g * grp + j) // 2
            tile = q[:, t * LANES:(t + 1) * LANES]
            lhs = jnp.where(lo, tile, zero) if j % 2 == 0 else jnp.where(lo, zero, tile)
            sink = jnp.full((l, 1), sink_ref[0, g * grp + j], F32)
            outs.append(_softmax_pv(_dot_nt(lhs, keys), sink, vals))
        for j in range(0, grp, 2):
            tiles.append(jnp.where(lo, outs[j], outs[j + 1]))
    z = z_ref[...]
    y_ref[...] = (jnp.concatenate(tiles, axis=1) * (z * jax.nn.sigmoid(z))).astype(y_ref.dtype)


def context_attn(bqc, bk2c, bv2c, bzc, sink_l):
    b, l, w = bqc.shape
    kw = bk2c.shape[2]
    kv = pl.BlockSpec((None, l, kw), lambda i: (i, 0, 0))
    qz = pl.BlockSpec((None, l, w), lambda i: (i, 0, 0))
    return pl.pallas_call(
        _cattn_kernel,
        grid=(b,),
        in_specs=[pl.BlockSpec(memory_space=pltpu.SMEM), qz, kv, kv, qz],
        out_specs=qz,
        out_shape=jax.ShapeDtypeStruct((b, l, w), BF16),
        compiler_params=_cparams(("arbitrary",)),
        name="context_attn",
    )(sink_l.reshape(1, -1), bqc, bk2c, bv2c, bzc)


CONV_PAD = 8
GROUPS_PER_STEP = 2
_R_GCF, _R_BF, _R_GCB, _R_BB, _R_TEF, _R_TEB = range(6)
INV_BASE = 8
N_MERGE = int(np.log2(A_CHUNK // INV_BASE))
_M_NEGF, _M_NEGB, _M_STRICTF, _M_STRICTB, _M_SAME, _M_EYES, _M_SAMEB, _M_OFF0 = range(8)
N_MASKS = _M_OFF0 + N_MERGE


def _delta_kernel(alog_ref, dtb_ref, cwq_ref, cwk_ref, cwv_ref, nw_ref,
                  xq_ref, xk_ref, xv_ref, z_ref, bat_ref,
                  xqc_ref, xkc_ref, xvc_ref, zc_ref, batc_ref,
                  *rest, ctx_out):
    if ctx_out:
        y_ref, yc_ref = rest[:2]
        rest = rest[2:]
    else:
        y_ref, yc_ref = rest[0], None
        rest = rest[1:]
    xp, qn, kn, vn, of, ob, gates, masks, p_buf, n_buf, qe_buf, ol_buf, tot_buf = rest
    h = pl.program_id(1)
    gsz = GROUP
    ck = A_CHUNK
    nck = gsz // ck
    dk = A_DK

    ri = lax.broadcasted_iota(jnp.int32, (gsz, gsz), 0)
    ci = lax.broadcasted_iota(jnp.int32, (gsz, gsz), 1)
    same = (ri // ck) == (ci // ck)
    masks[_M_NEGF] = jnp.where(jnp.logical_and(same, ri >= ci), 0.0, -jnp.inf)
    masks[_M_NEGB] = jnp.where(jnp.logical_and(same, ri <= ci), 0.0, -jnp.inf)
    masks[_M_STRICTF] = jnp.where(jnp.logical_and(same, ri > ci), 1.0, 0.0)
    masks[_M_STRICTB] = jnp.where(jnp.logical_and(same, ri < ci), 1.0, 0.0)
    masks[_M_SAME] = jnp.where(same, 1.0, 0.0)
    masks[_M_EYES] = jnp.where(ri == ci % ck, 1.0, 0.0)
    masks[_M_SAMEB] = jnp.where((ri // INV_BASE) == (ci // INV_BASE), 1.0, 0.0)
    for lvl in range(N_MERGE):
        m = INV_BASE << lvl
        masks[_M_OFF0 + lvl] = jnp.where(
            jnp.logical_and((ri // (2 * m)) == (ci // (2 * m)), (ri // m) != (ci // m)), 1.0, 0.0)

    lane8 = lax.broadcasted_iota(jnp.int32, alog_ref.shape, 1)
    a_all = jnp.exp(alog_ref[...])
    dt_all = dtb_ref[...]

    def pick(v, idx):
        return jnp.sum(jnp.where(lane8 == idx, v, 0.0), axis=1, keepdims=True)

    a_f, a_b = pick(a_all, h), pick(a_all, h + A_HEADS)
    dt_f, dt_b = pick(dt_all, h), pick(dt_all, h + A_HEADS)
    nw = nw_ref[...]

    def conv_silu(x_ref, cw_ref, n):
        zpad = jnp.zeros((CONV_PAD, dk), F32)
        xp[0:CONV_PAD, :] = zpad
        xp[CONV_PAD + n:2 * CONV_PAD + n, :] = zpad

        def cp(i, _):
            r0 = pl.multiple_of(i * gsz, gsz)
            xp[pl.ds(CONV_PAD + r0, gsz), :] = x_ref[pl.ds(r0, gsz), :]
            return 0
        lax.fori_loop(0, n // gsz, cp, 0)
        return cw_ref[...]

    def conv_tile(w, r0):
        acc = None
        for j in range(A_CONV):
            term = xp[pl.ds(r0 + (CONV_PAD - A_CONV // 2 + j), gsz), :] * w[j:j + 1, :]
            acc = term if acc is None else acc + term
        return acc * jax.nn.sigmoid(acc)

    def prologue(xq, xk, xv, bat, n):
        ng = n // gsz
        w = conv_silu(xq, cwq_ref, n)

        def bq(i, _):
            r0 = pl.multiple_of(i * gsz, gsz)
            s = conv_tile(w, r0)
            qn[pl.ds(r0, gsz), :] = (s * lax.rsqrt(jnp.sum(s * s, axis=-1, keepdims=True) + EPS)
                                     * (dk ** -0.5)).astype(BF16)
            return 0
        lax.fori_loop(0, ng, bq, 0)
        w = conv_silu(xk, cwk_ref, n)

        def bk(i, _):
            r0 = pl.multiple_of(i * gsz, gsz)
            s = conv_tile(w, r0)
            kn[pl.ds(r0, gsz), :] = (s * lax.rsqrt(jnp.sum(s * s, axis=-1, keepdims=True) + EPS)).astype(BF16)
            return 0
        lax.fori_loop(0, ng, bk, 0)
        w = conv_silu(xv, cwv_ref, n)

        def bv(i, _):
            r0 = pl.multiple_of(i * gsz, gsz)
            vn[pl.ds(r0, gsz), :] = conv_tile(w, r0).astype(BF16)
            return 0
        lax.fori_loop(0, ng, bv, 0)

        beta_f = jax.nn.sigmoid(bat[pl.ds(h, 1), :])
        beta_b = jax.nn.sigmoid(bat[pl.ds(h + A_HEADS, 1), :])
        g_f = -a_f * jax.nn.softplus(bat[pl.ds(h + 2 * A_HEADS, 1), :] + dt_f)
        g_b = -a_b * jax.nn.softplus(bat[pl.ds(h + 3 * A_HEADS, 1), :] + dt_b)
        row = lax.broadcasted_iota(jnp.int32, (8, n), 0)
        pos = lax.broadcasted_iota(jnp.int32, (8, n), 1) % ck
        g2 = jnp.where(row == 0, g_f, jnp.where(row == 1, g_b, 0.0))
        cf, cb = g2, g2
        s = 1
        while s < ck:
            cf = cf + jnp.where(pos >= s, pltpu.roll(cf, s, 1), 0.0)
            cb = cb + jnp.where(pos < ck - s, pltpu.roll(cb, n - s, 1), 0.0)
            s *= 2
        tab = jnp.where(row == _R_GCF, cf[0:1], 0.0)
        tab = jnp.where(row == _R_BF, beta_f, tab)
        tab = jnp.where(row == _R_GCB, cb[1:2], tab)
        tab = jnp.where(row == _R_BB, beta_b, tab)
        tab = jnp.where(row == _R_TEF, cb[0:1] - g_f, tab)
        tab = jnp.where(row == _R_TEB, cf[1:2] - g_b, tab)
        for g in range(ng):
            gates[g] = tab[:, g * gsz:(g + 1) * gsz]

    def block_diag(m_sbs):
        return jnp.concatenate([m_sbs] * nck, axis=0) * masks[_M_SAME]

    def local_chain(g, slot, j, d):
        fwd = d == 0
        r0 = pl.multiple_of(g * gsz, gsz)
        kg = kn[pl.ds(r0, gsz), :]
        qg = qn[pl.ds(r0, gsz), :]
        vg = vn[pl.ds(r0, gsz), :]
        rg = gates[g]
        rgt = rg.T
        i_gc, i_b, i_te = (_R_GCF, _R_BF, _R_TEF) if fwd else (_R_GCB, _R_BB, _R_TEB)
        gc_row, b_row, te_row = rg[i_gc:i_gc + 1], rg[i_b:i_b + 1], rg[i_te:i_te + 1]
        gc_col, b_col, te_col = rgt[:, i_gc:i_gc + 1], rgt[:, i_b:i_b + 1], rgt[:, i_te:i_te + 1]
        dec = jnp.exp((gc_col - gc_row) + masks[_M_NEGF if fwd else _M_NEGB])
        ab = _dot_nt(jnp.concatenate([kg, qg], axis=0), kg)
        yield
        lm = ab[:gsz] * b_col * dec * masks[_M_STRICTF if fwd else _M_STRICTB]
        qm = (ab[gsz:] * dec).astype(BF16)
        ld = lm * masks[_M_SAMEB]
        ld_sbs = ld[0:ck]
        for c in range(1, nck):
            ld_sbs = ld_sbs + ld[c * ck:(c + 1) * ck]
        x = masks[_M_EYES, 0:ck, :] - ld_sbs
        p = jnp.dot(ld_sbs.astype(BF16), ld.astype(BF16), preferred_element_type=F32)
        yield
        order = 2
        while order < INV_BASE:
            pbd = block_diag(p).astype(BF16)
            if 2 * order < INV_BASE:
                r = jnp.dot(jnp.concatenate([x.astype(BF16), p.astype(BF16)], axis=0), pbd,
                            preferred_element_type=F32)
                yield
                x = x + r[:ck]
                p = r[ck:]
            else:
                r = jnp.dot(x.astype(BF16), pbd, preferred_element_type=F32)
                yield
                x = x + r
            order *= 2
        for lvl in range(N_MERGE):
            y = jnp.dot(x.astype(BF16), (lm * masks[_M_OFF0 + lvl]).astype(BF16), preferred_element_type=F32)
            yield
            r = jnp.dot(y.astype(BF16), block_diag(x).astype(BF16), preferred_element_type=F32)
            yield
            x = x - r
        tb = block_diag(x) * b_row
        tbg = tb * jnp.exp(gc_row)
        uw = jnp.dot(jnp.concatenate([tb.astype(BF16), tbg.astype(BF16)], axis=0),
                     jnp.concatenate([vg, kg], axis=1), preferred_element_type=F32)
        yield
        wu = jnp.concatenate([uw[gsz:, dk:], uw[:gsz, :dk]], axis=1).astype(BF16)
        qwu = jnp.dot(qm, wu, preferred_element_type=F32)
        kt = (kg.astype(F32) * jnp.exp(te_col)).astype(BF16)
        pn = [_dot_tn(kt[c * ck:(c + 1) * ck], wu[c * ck:(c + 1) * ck]) for c in range(nck)]
        yield
        ol_buf[slot, d, j] = qwu[:, dk:]
        qe_buf[slot, d, j] = (qg.astype(F32) * jnp.exp(gc_col) - qwu[:, :dk]).astype(BF16)
        for c in range(nck):
            p_buf[slot, d, j, c] = pn[c][:, :dk].astype(BF16)
            n_buf[slot, d, j, c] = pn[c][:, dk:]
        tot_buf[slot, d, j, 0:1, :] = jnp.exp(gc_row + te_row)

    def rec_chain(groups, slot, d, s, out):
        fwd = d == 0
        oacc = of if fwd else ob
        for j, g in enumerate(groups):
            r0 = pl.multiple_of(g * gsz, gsz)
            tot = tot_buf[slot, d, j, 0:1, :]
            for c in (range(nck) if fwd else range(nck - 1, -1, -1)):
                lhs = jnp.concatenate([p_buf[slot, d, j, c], qe_buf[slot, d, j, c * ck:(c + 1) * ck, :]], axis=0)
                ps = jnp.dot(lhs, s.astype(BF16), preferred_element_type=F32)
                yield
                oacc[pl.ds(r0 + c * ck, ck), :] = ol_buf[slot, d, j, c * ck:(c + 1) * ck, :] + ps[dk:]
                s = s * tot[:, c * ck:c * ck + 1] - ps[:dk] + n_buf[slot, d, j, c]
        out[d] = s

    def run_interleaved(chains):
        active = list(chains)
        while active:
            for ch in list(active):
                try:
                    next(ch)
                except StopIteration:
                    active.remove(ch)

    def sweeps(n, s_f, s_b):
        ng = n // gsz
        gps = GROUPS_PER_STEP if ng % GROUPS_PER_STEP == 0 else 1
        nsteps = ng // gps
        groups_f = lambda i: [i * gps + j for j in range(gps)]
        groups_b = lambda i: [ng - 1 - (i * gps + j) for j in range(gps)]

        def locals_of(i, slot):
            return ([local_chain(g, slot, j, 0) for j, g in enumerate(groups_f(i))]
                    + [local_chain(g, slot, j, 1) for j, g in enumerate(groups_b(i))])

        def recs_of(i, slot, s_f, s_b, out):
            return [rec_chain(groups_f(i), slot, 0, s_f, out), rec_chain(groups_b(i), slot, 1, s_b, out)]

        run_interleaved(locals_of(0, 0))

        def body(i, carry):
            out = [None, None]
            slot = i % 2
            run_interleaved(locals_of(i, slot) + recs_of(i - 1, 1 - slot, carry[0], carry[1], out))
            return tuple(out)
        s_f, s_b = lax.fori_loop(1, nsteps, body, (s_f, s_b))
        out = [None, None]
        run_interleaved(recs_of(nsteps - 1, (nsteps - 1) % 2, s_f, s_b, out))
        return tuple(out)

    def epilogue(zr, yr, n):
        def body(i, _):
            r0 = pl.multiple_of(i * gsz, gsz)
            o = of[pl.ds(r0, gsz), :] + ob[pl.ds(r0, gsz), :]
            zg = zr[pl.ds(r0, gsz), :]
            yr[pl.ds(r0, gsz), :] = (o * lax.rsqrt(jnp.mean(o * o, axis=-1, keepdims=True) + EPS) * nw
                                     * (zg * jax.nn.sigmoid(zg))).astype(yr.dtype)
            return 0
        lax.fori_loop(0, n // gsz, body, 0)

    n_c, n_x = xqc_ref.shape[0], xq_ref.shape[0]
    zeros = jnp.zeros((dk, dk), F32)
    prologue(xqc_ref, xkc_ref, xvc_ref, batc_ref, n_c)
    s_f, s_b = sweeps(n_c, zeros, zeros)
    if ctx_out:
        epilogue(zc_ref, yc_ref, n_c)
    prologue(xq_ref, xk_ref, xv_ref, bat_ref, n_x)
    sweeps(n_x, s_f, s_b)
    epilogue(z_ref, y_ref, n_x)


def delta_mixer(aqkv, az, bat, aqkvc, azc, batc, conv_w_l, a_log_l, dt_bias_l, norm_w_l, ctx_out):
    b, t, _ = aqkv.shape
    l = aqkvc.shape[1]
    assert t % GROUP == 0 and l % GROUP == 0
    dk = A_DK
    nh = A_HEADS
    col = lambda n, off: pl.BlockSpec((None, n, dk), lambda i, j: (i, 0, j + off))
    cw = lambda off: pl.BlockSpec((A_CONV, dk), lambda i, j: (0, j + off))
    small = pl.BlockSpec((1, 2 * nh), lambda i, j: (0, 0))
    gate = lambda n: pl.BlockSpec((None, 4 * nh, n), lambda i, j: (i, 0, 0))
    out_shape = [jax.ShapeDtypeStruct((b, t, nh * dk), BF16)]
    out_specs = [col(t, 0)]
    if ctx_out:
        out_shape.append(jax.ShapeDtypeStruct((b, l, nh * dk), BF16))
        out_specs.append(col(l, 0))
    nmax = max(t, l)
    nck = GROUP // A_CHUNK
    lead = (2, 2, GROUPS_PER_STEP)
    handover = [pltpu.VMEM(lead + (nck, dk, dk), BF16), pltpu.VMEM(lead + (nck, dk, dk), F32),
                pltpu.VMEM(lead + (GROUP, dk), BF16), pltpu.VMEM(lead + (GROUP, dk), F32),
                pltpu.VMEM(lead + (8, GROUP), F32)]
    res = pl.pallas_call(
        functools.partial(_delta_kernel, ctx_out=ctx_out),
        grid=(b, nh),
        in_specs=[small, small, cw(0), cw(nh), cw(2 * nh), pl.BlockSpec((1, dk), lambda i, j: (0, 0)),
                  col(t, 0), col(t, nh), col(t, 2 * nh), col(t, 0), gate(t),
                  col(l, 0), col(l, nh), col(l, 2 * nh), col(l, 0), gate(l)],
        out_specs=out_specs,
        out_shape=out_shape,
        scratch_shapes=[pltpu.VMEM((nmax + 2 * CONV_PAD, dk), F32),
                        pltpu.VMEM((nmax, dk), BF16), pltpu.VMEM((nmax, dk), BF16), pltpu.VMEM((nmax, dk), BF16),
                        pltpu.VMEM((nmax, dk), F32), pltpu.VMEM((nmax, dk), F32),
                        pltpu.VMEM((nmax // GROUP, 8, GROUP), F32),
                        pltpu.VMEM((N_MASKS, GROUP, GROUP), F32)] + handover,
        compiler_params=_cparams(("arbitrary", "arbitrary")),
        name="delta_mixer",
    )(a_log_l.reshape(1, -1), dt_bias_l.reshape(1, -1), conv_w_l, conv_w_l, conv_w_l, norm_w_l.reshape(1, -1),
      aqkv, aqkv, aqkv, az, bat, aqkvc, aqkvc, aqkvc, azc, batc)
    return res if ctx_out else (res[0], None)


def _rope_angles(pos, n_freq):
    inv = ROPE_BASE ** (-jnp.arange(n_freq, dtype=F32) / n_freq)
    return pos[:, None] * inv[None, :]


def _position_tables(t):
    rows_n = t // GRID_W
    rows = jnp.repeat(jnp.arange(rows_n, dtype=F32), GRID_W)
    cols = jnp.tile(jnp.arange(GRID_W, dtype=F32), rows_n)
    n_ax = B_HD // 4
    ar, ac = _rope_angles(rows, n_ax), _rope_angles(cols, n_ax)
    cr, sr, cc, sc = jnp.cos(ar), jnp.sin(ar), jnp.cos(ac), jnp.sin(ac)
    zz = jnp.zeros_like(sr)
    reps = LANES // B_HD
    cosa = jnp.tile(jnp.concatenate([cr, cr, cc, cc], axis=1), (1, reps))
    sinm = jnp.tile(jnp.concatenate([-sr, zz, -sc, zz], axis=1), (1, reps))
    sinp = jnp.tile(jnp.concatenate([zz, sr, zz, sc], axis=1), (1, reps))
    at = _rope_angles(jnp.arange(t, dtype=F32), C_HD // 2)
    cosr = jnp.concatenate([jnp.cos(at), jnp.cos(at)], axis=1)
    sinr = jnp.concatenate([-jnp.sin(at), jnp.sin(at)], axis=1)
    return (cosa, sinm, sinp, cosr, sinr)


def _identity_tables(l):
    one, zero = jnp.ones((l, LANES), F32), jnp.zeros((l, LANES), F32)
    return (one, zero, zero, one, zero)


def _split_weights(w):
    sizes = (3 * A_WIDTH, A_WIDTH, 2 * A_HEADS, 2 * A_HEADS, B_Q_HEADS * B_HD, 2 * B_KV_HEADS * B_HD, BR_WIDTH,
             3 * BR_WIDTH, BR_WIDTH, N_BRANCH * D_MODEL)
    pts = np.cumsum(sizes)[:-1]
    aqkv, az, abeta, aalpha, bq, bkv, bz, cqkv, cz, mg = jnp.split(w, pts, axis=1)
    ba = jnp.concatenate([abeta, aalpha], axis=1)
    ba = jnp.pad(ba, ((0, 0), (0, LANES - ba.shape[1])))
    heads = [bkv[:, i * B_HD:(i + 1) * B_HD] for i in range(2 * B_KV_HEADS)]
    bkv2 = jnp.concatenate([hd for hd in heads for _ in range(LANES // B_HD)], axis=1)
    return [a.astype(BF16) for a in (aqkv, az, ba, bq, bkv2, bz, cqkv, cz, mg)]


def kernel(x, c, ctx, c_ctx, w_ada, b_ada, norm_w, w_in, a_conv_w, a_log, a_dt_bias, a_norm_w, b_sink, c_decay,
           c_norm_w, w_branch, w_out, final_norm_w):
    b, t, d = x.shape
    l = ctx.shape[1]
    depth = w_ada.shape[0]
    assert d == D_MODEL and t % INPROJ_TM == 0 and l % INPROJ_TM == 0
    assert t % MERGE_TM == 0 and (b * l) % MERGE_TM == 0 and t % GRID_W == 0

    n_mod = -(-(b + 1) // 8) * 8
    cc = jnp.concatenate([c, c_ctx[None, :], jnp.zeros((n_mod - b - 1, d), F32)], axis=0)
    mod_all = ada_mod(cc, w_ada, b_ada)
    x_tables = _position_tables(t)
    c_tables = _identity_tables(l)

    x2d = x.reshape(b * t, d)
    c2d = ctx.reshape(b * l, d)
    for layer in range(depth):
        last = layer == depth - 1
        mod = mod_all[layer].reshape(n_mod, 1, 3 * d)
        weights = _split_weights(w_in[layer])
        nw = norm_w[layer].reshape(1, d)
        px = in_proj(x2d, mod, lambda i: i // (t // INPROJ_TM), nw, x_tables, weights, t)
        pc = in_proj(c2d, mod, lambda i: b, nw, c_tables, weights, l)
        (aqkv, az, ba, bq, bk, bv, bz, cq, ck, cv, cz, mg) = [a.reshape(b, t, -1) for a in px]
        (aqkvc, azc, bac, bqc, bkc, bvc, bzc, cqc, ckc, cvc, czc, mgc) = [a.reshape(b, l, -1) for a in pc]
        bat = jnp.swapaxes(ba[:, :, :4 * A_HEADS], 1, 2)
        batc = jnp.swapaxes(bac[:, :, :4 * A_HEADS], 1, 2)

        ya, yac = delta_mixer(aqkv, az, bat, aqkvc, azc, batc, a_conv_w[layer], a_log[layer], a_dt_bias[layer],
                              a_norm_w[layer], not last)
        yb = window_attn(bq, bk, bv, bz, bkc, bvc, b_sink[layer])
        yc, ycc = retention(cq, ck, cv, cz, cqc, ckc, cvc, czc, c_decay[layer], c_norm_w[layer], not last)

        wbr = w_branch[layer].astype(BF16)
        wo = w_out[layer].astype(BF16)
        fw = final_norm_w.reshape(1, d)
        flat = lambda a: a.reshape(-1, a.shape[-1])
        x2d_new = merge_out(flat(ya), flat(yb), flat(yc), flat(mg), x2d, mod, lambda i: i // (t // MERGE_TM),
                            wbr, wo, fw, last)
        if not last:
            ybc = context_attn(bqc, bkc, bvc, bzc, b_sink[layer])
            c2d = merge_out(flat(yac), flat(ybc), flat(ycc), flat(mgc), c2d, mod, lambda i: b, wbr, wo, fw, False)
        x2d = x2d_new
    return x2d.reshape(b, t, d)
```

```python
import functools

import jax
import jax.numpy as jnp
import numpy as np
from jax import lax
from jax.experimental import pallas as pl
from jax.experimental.pallas import tpu as pltpu

F32 = jnp.float32
BF16 = jnp.bfloat16

D_MODEL = 1024
GRID_W = 64
EPS = 1e-6
ROPE_BASE = 10000.0
BR_WIDTH = D_MODEL // 2
A_DK = 128
A_HEADS = BR_WIDTH // A_DK
A_WIDTH = A_HEADS * A_DK
A_CONV = 5
A_CHUNK = 64
B_HD = 64
B_Q_HEADS = BR_WIDTH // B_HD
B_KV_HEADS = B_Q_HEADS // 4
B_BLOCK = 128
C_HD = 128
C_HEADS = BR_WIDTH // C_HD
N_BRANCH = 3

LANES = 128
GROUP = 256
VMEM_LIMIT = 56 * 1024 * 1024


def _cparams(sem):
    return pltpu.CompilerParams(dimension_semantics=sem, vmem_limit_bytes=VMEM_LIMIT)


def _const_spec(shape):
    nd = len(shape)
    return pl.BlockSpec(shape, lambda *_: (0,) * nd, pipeline_mode=pl.Buffered(1))


def _ada_kernel(c_ref, w_ref, b_ref, o_ref):
    c = c_ref[...]
    s = c * jax.nn.sigmoid(c)
    o_ref[...] = jnp.dot(s, w_ref[...], preferred_element_type=F32,
                         precision=lax.Precision.HIGHEST) + b_ref[...]


def ada_mod(cc, w_ada, b_ada):
    depth = w_ada.shape[0]
    r, d = cc.shape
    tn = 1024
    nt = (3 * d) // tn
    return pl.pallas_call(
        _ada_kernel,
        grid=(depth, nt),
        in_specs=[pl.BlockSpec((r, d), lambda l, j: (0, 0)),
                  pl.BlockSpec((None, d, tn), lambda l, j: (l, 0, j)),
                  pl.BlockSpec((None, 1, tn), lambda l, j: (l, 0, j))],
        out_specs=pl.BlockSpec((None, r, tn), lambda l, j: (l, 0, j)),
        out_shape=jax.ShapeDtypeStruct((depth, r, 3 * d), F32),
        compiler_params=_cparams(("arbitrary", "arbitrary")),
        name="ada_mod",
    )(cc, w_ada, b_ada.reshape(depth, 1, 3 * d))


def _rot_axial(x, cosa, sinm, sinp):
    w = x.shape[-1]
    reps = w // LANES
    ca = jnp.concatenate([cosa] * reps, axis=-1)
    sm = jnp.concatenate([sinm] * reps, axis=-1)
    sp = jnp.concatenate([sinp] * reps, axis=-1)
    return x * ca + pltpu.roll(x, w - 16, 1) * sm + pltpu.roll(x, 16, 1) * sp


def _rot_half128(x, cosr, sinr):
    outs = []
    for t in range(x.shape[-1] // LANES):
        xt = x[:, t * LANES:(t + 1) * LANES]
        outs.append(xt * cosr + pltpu.roll(xt, LANES // 2, 1) * sinr)
    return jnp.concatenate(outs, axis=-1)


def _inproj_kernel(x_ref, mod_ref, nw_ref, cosa_ref, sinm_ref, sinp_ref, cosr_ref, sinr_ref,
                   w_aqkv, w_az, w_ba, w_bq, w_bkv, w_bz, w_cqkv, w_cz, w_mg,
                   o_aqkv, o_az, o_ba, o_bq, o_bk, o_bv, o_bz, o_cq, o_ck, o_cv, o_cz, o_mg):
    d = D_MODEL
    x = x_ref[...]
    y = x * lax.rsqrt(jnp.mean(x * x, axis=-1, keepdims=True) + EPS) * nw_ref[...]
    mod = mod_ref[...]
    h = (y * (1.0 + mod[:, d:2 * d]) + mod[:, :d]).astype(BF16)

    def proj(w):
        return jnp.dot(h, w[...], preferred_element_type=F32)

    o_aqkv[...] = proj(w_aqkv)
    o_az[...] = proj(w_az)
    o_ba[...] = proj(w_ba)
    cosa, sinm, sinp = cosa_ref[...], sinm_ref[...], sinp_ref[...]
    o_bq[...] = (_rot_axial(proj(w_bq), cosa, sinm, sinp) * (B_HD ** -0.5)).astype(BF16)
    bkv = proj(w_bkv)
    half = bkv.shape[-1] // 2
    o_bk[...] = _rot_axial(bkv[:, :half], cosa, sinm, sinp).astype(BF16)
    o_bv[...] = bkv[:, half:].astype(BF16)
    o_bz[...] = proj(w_bz)
    cqkv = proj(w_cqkv)
    cw = cqkv.shape[-1] // 3
    cosr, sinr = cosr_ref[...], sinr_ref[...]
    o_cq[...] = _rot_half128(cqkv[:, :cw], cosr, sinr).astype(BF16)
    o_ck[...] = (_rot_half128(cqkv[:, cw:2 * cw], cosr, sinr) * (C_HD ** -0.5)).astype(BF16)
    o_cv[...] = cqkv[:, 2 * cw:].astype(BF16)
    o_cz[...] = proj(w_cz)
    o_mg[...] = proj(w_mg).astype(BF16)


INPROJ_TM = 256


def in_proj(x2d, mod, mod_row_of_tile, nw, tables, weights, seq_len):
    r, d = x2d.shape
    tm = INPROJ_TM
    tiles_per_seq = seq_len // tm
    row = lambda i: (i, 0)
    tab = lambda i: (i % tiles_per_seq, 0)
    in_specs = [pl.BlockSpec((tm, d), row),
                pl.BlockSpec((None, 1, 3 * d), lambda i: (mod_row_of_tile(i), 0, 0)),
                _const_spec((1, d))]
    in_specs += [pl.BlockSpec((tm, LANES), tab) for _ in range(5)]
    in_specs += [_const_spec(w.shape) for w in weights]
    widths = [(w.shape[1], F32) for w in weights]
    (aqkv, az, ba, bq, bkv, bz, cqkv, cz, mg) = [w.shape[1] for w in weights]
    outs = [(aqkv, F32), (az, F32), (ba, F32), (bq, BF16), (bkv // 2, BF16), (bkv // 2, BF16), (bz, F32),
            (cqkv // 3, BF16), (cqkv // 3, BF16), (cqkv // 3, BF16), (cz, F32), (mg, BF16)]
    del widths
    return pl.pallas_call(
        _inproj_kernel,
        grid=(r // tm,),
        in_specs=in_specs,
        out_specs=[pl.BlockSpec((tm, n), row) for n, _ in outs],
        out_shape=[jax.ShapeDtypeStruct((r, n), dt) for n, dt in outs],
        compiler_params=_cparams(("arbitrary",)),
        name="in_proj",
    )(x2d, mod, nw, *tables, *weights)


def _merge_kernel(ya_ref, yb_ref, yc_ref, mg_ref, x_ref, mod_ref, wbr_ref, wo_ref, fw_ref, o_ref, *, final):
    d = D_MODEL
    merged = None
    for i, y_ref in enumerate((ya_ref, yb_ref, yc_ref)):
        gate_i = jax.nn.sigmoid(mg_ref[:, i * d:(i + 1) * d].astype(F32))
        term = gate_i * jnp.dot(y_ref[...], wbr_ref[i], preferred_element_type=F32)
        merged = term if merged is None else merged + term
    out = jnp.dot(merged.astype(BF16), wo_ref[...], preferred_element_type=F32)
    xn = x_ref[...] + mod_ref[:, 2 * d:] * out
    if final:
        xn = xn * lax.rsqrt(jnp.mean(xn * xn, axis=-1, keepdims=True) + EPS) * fw_ref[...]
    o_ref[...] = xn


MERGE_TM = 512


def merge_out(ya, yb, yc, mg, x2d, mod, mod_row_of_tile, wbr, wo, fw, final):
    r, d = x2d.shape
    tm = MERGE_TM
    row = lambda i: (i, 0)
    bw = ya.shape[1]
    return pl.pallas_call(
        functools.partial(_merge_kernel, final=final),
        grid=(r // tm,),
        in_specs=[pl.BlockSpec((tm, bw), row), pl.BlockSpec((tm, bw), row), pl.BlockSpec((tm, bw), row),
                  pl.BlockSpec((tm, N_BRANCH * d), row), pl.BlockSpec((tm, d), row),
                  pl.BlockSpec((None, 1, 3 * d), lambda i: (mod_row_of_tile(i), 0, 0)),
                  _const_spec(wbr.shape), _const_spec(wo.shape), _const_spec((1, d))],
        out_specs=pl.BlockSpec((tm, d), row),
        out_shape=jax.ShapeDtypeStruct((r, d), F32),
        compiler_params=_cparams(("arbitrary",)),
        name="merge_out",
    )(ya, yb, yc, mg, x2d, mod, wbr, wo, fw)


def _dot_nt(a, b):
    return lax.dot_general(a, b, (((1,), (1,)), ((), ())), preferred_element_type=F32)


def _dot_tn(a, b):
    return lax.dot_general(a, b, (((0,), (0,)), ((), ())), preferred_element_type=F32)


def _run_interleaved(chains):
    active = list(chains)
    while active:
        for ch in list(active):
            try:
                next(ch)
            except StopIteration:
                active.remove(ch)


RET_GROUPS_PER_STEP = 4


def _retention_kernel(cd_ref, q_ref, k_ref, v_ref, z_ref, qc_ref, kc_ref, vc_ref, zc_ref, nw_ref,
                      *rest, ctx_out):
    if ctx_out:
        y_ref, yc_ref, u_scr, rhs_scr = rest
    else:
        (y_ref, u_scr, rhs_scr), yc_ref = rest, None
    h = pl.program_id(1)
    c = GROUP
    dh = q_ref.shape[-1]
    cd = cd_ref[...]
    lgv = jax.nn.log_sigmoid(cd)
    lane = lax.broadcasted_iota(jnp.int32, cd.shape, 1)
    lgf = jnp.sum(jnp.where(lane == h, lgv, 0.0), axis=1, keepdims=True)
    lgb = jnp.sum(jnp.where(lane == h + C_HEADS, lgv, 0.0), axis=1, keepdims=True)
    dij = (lax.broadcasted_iota(jnp.int32, (c, c), 0) - lax.broadcasted_iota(jnp.int32, (c, c), 1)).astype(F32)
    dmask = (jnp.where(dij >= 0, jnp.exp(jnp.maximum(dij, 0.0) * lgf), 0.0)
             + jnp.where(dij <= 0, jnp.exp(jnp.maximum(-dij, 0.0) * lgb), 0.0))
    rr = lax.broadcasted_iota(jnp.int32, (c, dh), 0).astype(F32)
    qdf = jnp.exp((rr + 1.0) * lgf)
    qdb = jnp.exp((c - rr) * lgb)
    kdf = jnp.exp((c - 1.0 - rr) * lgf)
    kdb = jnp.exp(rr * lgb)
    cdf = jnp.exp(c * lgf)
    cdb = jnp.exp(c * lgb)
    nw = nw_ref[...]

    def grouped_loop(ng, make_chain):
        per = RET_GROUPS_PER_STEP if ng % RET_GROUPS_PER_STEP == 0 else 1

        def body(i, _):
            _run_interleaved([make_chain(i * per + j) for j in range(per)])
            return 0
        lax.fori_loop(0, ng // per, body, 0)

    def state_pass(kr, vr, ng, sf, sb):
        def update_chain(g):
            r0 = pl.multiple_of(g * c, c)
            kg = kr[pl.ds(r0, c), :].astype(F32)
            lhs = jnp.concatenate([(kg * kdf).astype(BF16), (kg * kdb).astype(BF16)], axis=1)
            u = _dot_tn(lhs, vr[pl.ds(r0, c), :])
            yield
            u_scr[g] = u
        grouped_loop(ng, update_chain)

        def fbody(g, s):
            rhs_scr[g, 0:dh, :] = s.astype(BF16)
            return s * cdf + u_scr[g, 0:dh, :]
        sf = lax.fori_loop(0, ng, fbody, sf)

        def bbody(t, s):
            g = ng - 1 - t
            rhs_scr[g, dh:2 * dh, :] = s.astype(BF16)
            return s * cdb + u_scr[g, dh:2 * dh, :]
        sb = lax.fori_loop(0, ng, bbody, sb)
        return sf, sb

    def output_pass(qr, kr, vr, zr, yr, ng):
        def out_chain(g):
            r0 = pl.multiple_of(g * c, c)
            qg = qr[pl.ds(r0, c), :]
            sc = _dot_nt(qg, kr[pl.ds(r0, c), :])
            yield
            qf = qg.astype(F32)
            lhs = jnp.concatenate([(sc * dmask).astype(BF16), (qf * qdf).astype(BF16), (qf * qdb).astype(BF16)],
                                  axis=1)
            rhs = jnp.concatenate([vr[pl.ds(r0, c), :], rhs_scr[g]], axis=0)
            o = jnp.dot(lhs, rhs, preferred_element_type=F32)
            yield
            mu = jnp.mean(o, axis=-1, keepdims=True)
            oc = o - mu
            var = jnp.mean(oc * oc, axis=-1, keepdims=True)
            zg = zr[pl.ds(r0, c), :].astype(F32)
            yr[pl.ds(r0, c), :] = (oc * lax.rsqrt(var + EPS) * nw * (zg * jax.nn.sigmoid(zg))).astype(yr.dtype)
        grouped_loop(ng, out_chain)

    zeros = jnp.zeros((dh, dh), F32)
    n_c = qc_ref.shape[0] // c
    n_x = q_ref.shape[0] // c
    s_cf, s_cb = state_pass(kc_ref, vc_ref, n_c, zeros, zeros)
    if ctx_out:
        output_pass(qc_ref, kc_ref, vc_ref, zc_ref, yc_ref, n_c)
    state_pass(k_ref, v_ref, n_x, s_cf, s_cb)
    output_pass(q_ref, k_ref, v_ref, z_ref, y_ref, n_x)


def retention(cq, ck, cv, cz, cqc, ckc, cvc, czc, c_decay_l, c_norm_w_l, ctx_out):
    b, t, w = cq.shape
    l = cqc.shape[1]
    assert t % GROUP == 0 and l % GROUP == 0
    dh = C_HD
    seq = lambda n: pl.BlockSpec((None, n, dh), lambda i, j: (i, 0, j))
    out_shape = [jax.ShapeDtypeStruct((b, t, w), BF16)]
    out_specs = [seq(t)]
    if ctx_out:
        out_shape.append(jax.ShapeDtypeStruct((b, l, w), BF16))
        out_specs.append(seq(l))
    res = pl.pallas_call(
        functools.partial(_retention_kernel, ctx_out=ctx_out),
        grid=(b, C_HEADS),
        in_specs=[pl.BlockSpec((1, 2 * C_HEADS), lambda i, j: (0, 0)),
                  seq(t), seq(t), seq(t), seq(t), seq(l), seq(l), seq(l), seq(l),
                  pl.BlockSpec((1, dh), lambda i, j: (0, j))],
        out_specs=out_specs,
        out_shape=out_shape,
        scratch_shapes=[pltpu.VMEM((max(t, l) // GROUP, 2 * dh, dh), F32),
                        pltpu.VMEM((max(t, l) // GROUP, 2 * dh, dh), BF16)],
        compiler_params=_cparams(("arbitrary", "arbitrary")),
        name="retention",
    )(c_decay_l.reshape(1, -1), cq, ck, cv, cz, cqc, ckc, cvc, czc, c_norm_w_l.reshape(1, -1))
    return res if ctx_out else (res[0], None)


def _softmax_pv(s, sink, vals):
    m = jnp.maximum(jnp.max(s, axis=-1, keepdims=True), sink)
    p = jnp.exp(s - m)
    den = jnp.sum(p, axis=-1, keepdims=True) + jnp.exp(sink - m)
    return jnp.dot(p.astype(BF16), vals, preferred_element_type=F32) * pl.reciprocal(den)


def _wattn_kernel(sink_ref, q_ref, kp_ref, kc_ref, kn_ref, vp_ref, vc_ref, vn_ref, kx_ref, vx_ref, z_ref, y_ref):
    n = pl.program_id(1)
    nb = pl.num_programs(1)
    blk = B_BLOCK
    l = kx_ref.shape[0]
    grp = B_Q_HEADS // B_KV_HEADS
    q = q_ref[...]
    lo = lax.broadcasted_iota(jnp.int32, (blk, LANES), 1) < B_HD
    ri = lax.broadcasted_iota(jnp.int32, (blk, blk), 0)
    ci = lax.broadcasted_iota(jnp.int32, (blk, blk), 1)
    ok_prev = jnp.logical_and(ci >= ri, n > 0)
    ok_next = jnp.logical_and(ci <= ri, n < nb - 1)
    zero = jnp.zeros((blk, LANES), q.dtype)
    tiles = []
    for g in range(B_KV_HEADS):
        gs = slice(g * LANES, (g + 1) * LANES)
        keys = jnp.concatenate([kx_ref[:, gs], kp_ref[:, gs], kc_ref[:, gs], kn_ref[:, gs]], axis=0)
        vals = jnp.concatenate([vx_ref[:, gs], vp_ref[:, gs], vc_ref[:, gs], vn_ref[:, gs]], axis=0)
        rows = []
        for j in range(grp):
            t = (g * grp + j) // 2
            tile = q[:, t * LANES:(t + 1) * LANES]
            rows.append(jnp.where(lo, tile, zero) if j % 2 == 0 else jnp.where(lo, zero, tile))
        s_all = _dot_nt(jnp.concatenate(rows, axis=0), keys)
        outs = []
        for j in range(grp):
            s = s_all[j * blk:(j + 1) * blk]
            s = jnp.concatenate([s[:, :l],
                                 jnp.where(ok_prev, s[:, l:l + blk], -jnp.inf),
                                 s[:, l + blk:l + 2 * blk],
                                 jnp.where(ok_next, s[:, l + 2 * blk:], -jnp.inf)], axis=1)
            sink = jnp.full((blk, 1), sink_ref[0, g * grp + j], F32)
            outs.append(_softmax_pv(s, sink, vals))
        for j in range(0, grp, 2):
            tiles.append(jnp.where(lo, outs[j], outs[j + 1]))
    z = z_ref[...]
    y_ref[...] = (jnp.concatenate(tiles, axis=1) * (z * jax.nn.sigmoid(z))).astype(y_ref.dtype)


def window_attn(bq, bk2, bv2, bz, bk2c, bv2c, sink_l):
    b, t, w = bq.shape
    l = bk2c.shape[1]
    blk = B_BLOCK
    nb = t // blk
    kw = bk2.shape[2]
    prev = pl.BlockSpec((None, blk, kw), lambda i, n: (i, jnp.maximum(n - 1, 0), 0))
    cur = pl.BlockSpec((None, blk, kw), lambda i, n: (i, n, 0))
    nxt = pl.BlockSpec((None, blk, kw), lambda i, n: (i, jnp.minimum(n + 1, nb - 1), 0))
    ctx = pl.BlockSpec((None, l, kw), lambda i, n: (i, 0, 0))
    qz = pl.BlockSpec((None, blk, w), lambda i, n: (i, n, 0))
    return pl.pallas_call(
        _wattn_kernel,
        grid=(b, nb),
        in_specs=[pl.BlockSpec(memory_space=pltpu.SMEM), qz, prev, cur, nxt, prev, cur, nxt, ctx, ctx, qz],
        out_specs=qz,
        out_shape=jax.ShapeDtypeStruct((b, t, w), BF16),
        compiler_params=_cparams(("arbitrary", "arbitrary")),
        name="window_attn",
    )(sink_l.reshape(1, -1), bq, bk2, bk2, bk2, bv2, bv2, bv2, bk2c, bv2c, bz)


def _cattn_kernel(sink_ref, q_ref, kx_ref, vx_ref, z_ref, y_ref):
    l = q_ref.shape[0]
    grp = B_Q_HEADS // B_KV_HEADS
    q = q_ref[...]
    lo = lax.broadcasted_iota(jnp.int32, (l, LANES), 1) < B_HD
    zero = jnp.zeros((l, LANES), q.dtype)
    tiles = []
    for g in range(B_KV_HEADS):
        gs = slice(g * LANES, (g + 1) * LANES)
        keys, vals = kx_ref[:, gs], vx_ref[:, gs]
        outs = []
        for j in range(grp):
            t = (g * grp + j) // 2
            tile = q[:, t * LANES:(t + 1) * LANES]
            lhs = jnp.where(lo, tile, zero) if j % 2 == 0 else jnp.where(lo, zero, tile)
            sink = jnp.full((l, 1), sink_ref[0, g * grp + j], F32)
            outs.append(_softmax_pv(_dot_nt(lhs, keys), sink, vals))
        for j in range(0, grp, 2):
            tiles.append(jnp.where(lo, outs[j], outs[j + 1]))
    z = z_ref[...]
    y_ref[...] = (jnp.concatenate(tiles, axis=1) * (z * jax.nn.sigmoid(z))).astype(y_ref.dtype)


def context_attn(bqc, bk2c, bv2c, bzc, sink_l):
    b, l, w = bqc.shape
    kw = bk2c.shape[2]
    kv = pl.BlockSpec((None, l, kw), lambda i: (i, 0, 0))
    qz = pl.BlockSpec((None, l, w), lambda i: (i, 0, 0))
    return pl.pallas_call(
        _cattn_kernel,
        grid=(b,),
        in_specs=[pl.BlockSpec(memory_space=pltpu.SMEM), qz, kv, kv, qz],
        out_specs=qz,
        out_shape=jax.ShapeDtypeStruct((b, l, w), BF16),
        compiler_params=_cparams(("arbitrary",)),
        name="context_attn",
    )(sink_l.reshape(1, -1), bqc, bk2c, bv2c, bzc)


CONV_PAD = 8
GROUPS_PER_STEP = 2
CONV_TILES_PER_STEP = 4
_R_GCF, _R_BF, _R_GCB, _R_BB, _R_TEF, _R_TEB = range(6)
INV_BASE = 8
N_MERGE = int(np.log2(A_CHUNK // INV_BASE))
_M_NEGF, _M_NEGB, _M_STRICTF, _M_STRICTB, _M_SBS = range(5)
N_MASKS = 5
_B_SAME, _B_SAMEB, _B_OFF0 = range(3)
N_BMASKS = _B_OFF0 + N_MERGE


def _delta_kernel(alog_ref, dtb_ref, cwq_ref, cwk_ref, cwv_ref, nw_ref,
                  xq_ref, xk_ref, xv_ref, z_ref, bat_ref,
                  xqc_ref, xkc_ref, xvc_ref, zc_ref, batc_ref,
                  *rest, ctx_out):
    if ctx_out:
        y_ref, yc_ref = rest[:2]
        rest = rest[2:]
    else:
        y_ref, yc_ref = rest[0], None
        rest = rest[1:]
    xp, qn, kn, vn, of, ob, gates, masks, bmasks, p_buf, n_buf, qe_buf, ol_buf, tot_buf = rest
    h = pl.program_id(1)
    gsz = GROUP
    ck = A_CHUNK
    nck = gsz // ck
    dk = A_DK

    ri = lax.broadcasted_iota(jnp.int32, (gsz, gsz), 0)
    ci = lax.broadcasted_iota(jnp.int32, (gsz, gsz), 1)
    same = (ri // ck) == (ci // ck)
    masks[_M_NEGF] = jnp.where(jnp.logical_and(same, ri >= ci), 0.0, -jnp.inf)
    masks[_M_NEGB] = jnp.where(jnp.logical_and(same, ri <= ci), 0.0, -jnp.inf)
    masks[_M_STRICTF] = jnp.where(jnp.logical_and(same, ri > ci), 1.0, 0.0)
    masks[_M_STRICTB] = jnp.where(jnp.logical_and(same, ri < ci), 1.0, 0.0)
    cj = ci % ck
    masks[_M_SBS] = jnp.where(ri < ck, jnp.where(ri == cj, 1.0, 0.0),
                              jnp.where(((ri - ck) // INV_BASE) == (cj // INV_BASE), 1.0, 0.0))
    bmasks[_B_SAME] = jnp.where(same, 1.0, 0.0).astype(BF16)
    bmasks[_B_SAMEB] = jnp.where((ri // INV_BASE) == (ci // INV_BASE), 1.0, 0.0).astype(BF16)
    for lvl in range(N_MERGE):
        m = INV_BASE << lvl
        bmasks[_B_OFF0 + lvl] = jnp.where(
            jnp.logical_and((ri // (2 * m)) == (ci // (2 * m)), (ri // m) != (ci // m)), 1.0, 0.0).astype(BF16)

    lane8 = lax.broadcasted_iota(jnp.int32, alog_ref.shape, 1)
    a_all = jnp.exp(alog_ref[...])
    dt_all = dtb_ref[...]

    def pick(v, idx):
        return jnp.sum(jnp.where(lane8 == idx, v, 0.0), axis=1, keepdims=True)

    a_f, a_b = pick(a_all, h), pick(a_all, h + A_HEADS)
    dt_f, dt_b = pick(dt_all, h), pick(dt_all, h + A_HEADS)
    nw = nw_ref[...]

    def conv_pass(x_ref, cw_ref, dst, n, l2_scale):
        zpad = jnp.zeros((CONV_PAD, dk), F32)
        xp[0:CONV_PAD, :] = zpad
        xp[CONV_PAD + n:2 * CONV_PAD + n, :] = zpad
        ng = n // gsz

        def cp(i, _):
            r0 = pl.multiple_of(i * gsz, gsz)
            xp[pl.ds(CONV_PAD + r0, gsz), :] = x_ref[pl.ds(r0, gsz), :]
            return 0
        lax.fori_loop(0, ng, cp, 0)
        w = cw_ref[...]

        def tile_chain(g):
            r0 = pl.multiple_of(g * gsz, gsz)
            acc = None
            for j in range(A_CONV):
                term = xp[pl.ds(r0 + (CONV_PAD - A_CONV // 2 + j), gsz), :] * w[j:j + 1, :]
                acc = term if acc is None else acc + term
            s = acc * jax.nn.sigmoid(acc)
            if l2_scale is not None:
                ss = jnp.sum(s * s, axis=-1, keepdims=True)
                yield
                s = s * (lax.rsqrt(ss + EPS) * l2_scale)
            dst[pl.ds(r0, gsz), :] = s.astype(BF16)

        per = CONV_TILES_PER_STEP if ng % CONV_TILES_PER_STEP == 0 else 1

        def body(i, _):
            _run_interleaved([tile_chain(i * per + j) for j in range(per)])
            return 0
        lax.fori_loop(0, ng // per, body, 0)

    def prologue(xq, xk, xv, bat, n):
        ng = n // gsz
        conv_pass(xq, cwq_ref, qn, n, dk ** -0.5)
        conv_pass(xk, cwk_ref, kn, n, 1.0)
        conv_pass(xv, cwv_ref, vn, n, None)

        beta_f = jax.nn.sigmoid(bat[pl.ds(h, 1), :])
        beta_b = jax.nn.sigmoid(bat[pl.ds(h + A_HEADS, 1), :])
        g_f = -a_f * jax.nn.softplus(bat[pl.ds(h + 2 * A_HEADS, 1), :] + dt_f)
        g_b = -a_b * jax.nn.softplus(bat[pl.ds(h + 3 * A_HEADS, 1), :] + dt_b)
        row = lax.broadcasted_iota(jnp.int32, (8, n), 0)
        pos = lax.broadcasted_iota(jnp.int32, (8, n), 1) % ck
        g2 = jnp.where(row == 0, g_f, jnp.where(row == 1, g_b, 0.0))
        cf, cb = g2, g2
        s = 1
        while s < ck:
            cf = cf + jnp.where(pos >= s, pltpu.roll(cf, s, 1), 0.0)
            cb = cb + jnp.where(pos < ck - s, pltpu.roll(cb, n - s, 1), 0.0)
            s *= 2
        tab = jnp.where(row == _R_GCF, cf[0:1], 0.0)
        tab = jnp.where(row == _R_BF, beta_f, tab)
        tab = jnp.where(row == _R_GCB, cb[1:2], tab)
        tab = jnp.where(row == _R_BB, beta_b, tab)
        tab = jnp.where(row == _R_TEF, cb[0:1] - g_f, tab)
        tab = jnp.where(row == _R_TEB, cf[1:2] - g_b, tab)
        for g in range(ng):
            gates[g] = tab[:, g * gsz:(g + 1) * gsz]

    def block_diag(m_sbs):
        return jnp.concatenate([m_sbs.astype(BF16)] * nck, axis=0) * bmasks[_B_SAME]

    def local_chain(g, slot, j, d):
        fwd = d == 0
        r0 = pl.multiple_of(g * gsz, gsz)
        kg = kn[pl.ds(r0, gsz), :]
        qg = qn[pl.ds(r0, gsz), :]
        vg = vn[pl.ds(r0, gsz), :]
        rg = gates[g]
        rgt = rg.T
        i_gc, i_b, i_te = (_R_GCF, _R_BF, _R_TEF) if fwd else (_R_GCB, _R_BB, _R_TEB)
        gc_row, b_row, te_row = rg[i_gc:i_gc + 1], rg[i_b:i_b + 1], rg[i_te:i_te + 1]
        gc_col, b_col, te_col = rgt[:, i_gc:i_gc + 1], rgt[:, i_b:i_b + 1], rgt[:, i_te:i_te + 1]
        dec = jnp.exp((gc_col - gc_row) + masks[_M_NEGF if fwd else _M_NEGB])
        ab = _dot_nt(jnp.concatenate([kg, qg], axis=0), kg)
        yield
        lm = ab[:gsz] * b_col * dec * masks[_M_STRICTF if fwd else _M_STRICTB]
        qm = (ab[gsz:] * dec).astype(BF16)
        lmb = lm.astype(BF16)
        l_sbs = lm[0:ck]
        for c in range(1, nck):
            l_sbs = l_sbs + lm[c * ck:(c + 1) * ck]
        ld_sbs = l_sbs * masks[_M_SBS, ck:2 * ck, :]
        x = masks[_M_SBS, 0:ck, :] - ld_sbs
        p = jnp.dot(ld_sbs.astype(BF16), lmb * bmasks[_B_SAMEB], preferred_element_type=F32)
        yield
        order = 2
        while order < INV_BASE:
            pbd = block_diag(p)
            if 2 * order < INV_BASE:
                r = jnp.dot(jnp.concatenate([x.astype(BF16), p.astype(BF16)], axis=0), pbd,
                            preferred_element_type=F32)
                yield
                x = x + r[:ck]
                p = r[ck:]
            else:
                r = jnp.dot(x.astype(BF16), pbd, preferred_element_type=F32)
                yield
                x = x + r
            order *= 2
        for lvl in range(N_MERGE):
            y = jnp.dot(x.astype(BF16), lmb * bmasks[_B_OFF0 + lvl], preferred_element_type=F32)
            yield
            r = jnp.dot(y.astype(BF16), block_diag(x), preferred_element_type=F32)
            yield
            x = x - r
        kgc = (kg.astype(F32) * jnp.exp(gc_col)).astype(BF16)
        wu = jnp.dot(block_diag(x * b_row), jnp.concatenate([kgc, vg], axis=1), preferred_element_type=F32)
        yield
        wu = wu.astype(BF16)
        qwu = jnp.dot(qm, wu, preferred_element_type=F32)
        kt = (kg.astype(F32) * jnp.exp(te_col)).astype(BF16)
        pn = [_dot_tn(kt[c * ck:(c + 1) * ck], wu[c * ck:(c + 1) * ck]) for c in range(nck)]
        yield
        ol_buf[slot, d, j] = qwu[:, dk:]
        qe_buf[slot, d, j] = (qg.astype(F32) * jnp.exp(gc_col) - qwu[:, :dk]).astype(BF16)
        for c in range(nck):
            p_buf[slot, d, j, c] = pn[c][:, :dk].astype(BF16)
            n_buf[slot, d, j, c] = pn[c][:, dk:]
        tot_buf[slot, d, j, 0:1, :] = jnp.exp(gc_row + te_row)

    def rec_chain(groups, slot, d, s, out):
        fwd = d == 0
        oacc = of if fwd else ob
        for j, g in enumerate(groups):
            r0 = pl.multiple_of(g * gsz, gsz)
            tot = tot_buf[slot, d, j, 0:1, :]
            for c in (range(nck) if fwd else range(nck - 1, -1, -1)):
                lhs = jnp.concatenate([p_buf[slot, d, j, c], qe_buf[slot, d, j, c * ck:(c + 1) * ck, :]], axis=0)
                ps = jnp.dot(lhs, s.astype(BF16), preferred_element_type=F32)
                yield
                oacc[pl.ds(r0 + c * ck, ck), :] = ol_buf[slot, d, j, c * ck:(c + 1) * ck, :] + ps[dk:]
                s = s * tot[:, c * ck:c * ck + 1] - ps[:dk] + n_buf[slot, d, j, c]
        out[d] = s

    def sweep_plan(n):
        ng = n // gsz
        gps = GROUPS_PER_STEP if ng % GROUPS_PER_STEP == 0 else 1
        groups_f = lambda i: [i * gps + j for j in range(gps)]
        groups_b = lambda i: [ng - 1 - (i * gps + j) for j in range(gps)]

        def locals_of(i, slot):
            return ([local_chain(g, slot, j, 0) for j, g in enumerate(groups_f(i))]
                    + [local_chain(g, slot, j, 1) for j, g in enumerate(groups_b(i))])

        def recs_of(i, slot, s_f, s_b, out):
            return [rec_chain(groups_f(i), slot, 0, s_f, out), rec_chain(groups_b(i), slot, 1, s_b, out)]
        return locals_of, recs_of, ng // gps

    def pipelined(plan, s_f, s_b):
        locals_of, recs_of, nsteps = plan

        def body(i, carry):
            out = [None, None]
            slot = i % 2
            _run_interleaved(locals_of(i, slot) + recs_of(i - 1, 1 - slot, carry[0], carry[1], out))
            return tuple(out)
        s_f, s_b = lax.fori_loop(1, nsteps, body, (s_f, s_b))
        out = [None, None]
        _run_interleaved(recs_of(nsteps - 1, (nsteps - 1) % 2, s_f, s_b, out))
        return tuple(out)

    def epilogue(zr, yr, n):
        def body(i, _):
            r0 = pl.multiple_of(i * gsz, gsz)
            o = of[pl.ds(r0, gsz), :] + ob[pl.ds(r0, gsz), :]
            zg = zr[pl.ds(r0, gsz), :]
            yr[pl.ds(r0, gsz), :] = (o * lax.rsqrt(jnp.mean(o * o, axis=-1, keepdims=True) + EPS) * nw
                                     * (zg * jax.nn.sigmoid(zg))).astype(yr.dtype)
            return 0
        lax.fori_loop(0, n // gsz, body, 0)

    n_c, n_x = xqc_ref.shape[0], xq_ref.shape[0]
    zeros = jnp.zeros((dk, dk), F32)
    plan_c, plan_x = sweep_plan(n_c), sweep_plan(n_x)
    prologue(xqc_ref, xkc_ref, xvc_ref, batc_ref, n_c)
    if plan_c[2] == 1:
        _run_interleaved(plan_c[0](0, 1))
        prologue(xq_ref, xk_ref, xv_ref, bat_ref, n_x)
        out = [None, None]
        _run_interleaved(plan_c[1](0, 1, zeros, zeros, out) + plan_x[0](0, 0))
        s_f, s_b = out
        if ctx_out:
            epilogue(zc_ref, yc_ref, n_c)
    else:
        _run_interleaved(plan_c[0](0, 0))
        s_f, s_b = pipelined(plan_c, zeros, zeros)
        if ctx_out:
            epilogue(zc_ref, yc_ref, n_c)
        prologue(xq_ref, xk_ref, xv_ref, bat_ref, n_x)
        _run_interleaved(plan_x[0](0, 0))
    pipelined(plan_x, s_f, s_b)
    epilogue(z_ref, y_ref, n_x)


def delta_mixer(aqkv, az, bat, aqkvc, azc, batc, conv_w_l, a_log_l, dt_bias_l, norm_w_l, ctx_out):
    b, t, _ = aqkv.shape
    l = aqkvc.shape[1]
    assert t % GROUP == 0 and l % GROUP == 0
    dk = A_DK
    nh = A_HEADS
    col = lambda n, off: pl.BlockSpec((None, n, dk), lambda i, j: (i, 0, j + off))
    cw = lambda off: pl.BlockSpec((A_CONV, dk), lambda i, j: (0, j + off))
    small = pl.BlockSpec((1, 2 * nh), lambda i, j: (0, 0))
    gate = lambda n: pl.BlockSpec((None, 4 * nh, n), lambda i, j: (i, 0, 0))
    out_shape = [jax.ShapeDtypeStruct((b, t, nh * dk), BF16)]
    out_specs = [col(t, 0)]
    if ctx_out:
        out_shape.append(jax.ShapeDtypeStruct((b, l, nh * dk), BF16))
        out_specs.append(col(l, 0))
    nmax = max(t, l)
    nck = GROUP // A_CHUNK
    lead = (2, 2, GROUPS_PER_STEP)
    handover = [pltpu.VMEM(lead + (nck, dk, dk), BF16), pltpu.VMEM(lead + (nck, dk, dk), F32),
                pltpu.VMEM(lead + (GROUP, dk), BF16), pltpu.VMEM(lead + (GROUP, dk), F32),
                pltpu.VMEM(lead + (8, GROUP), F32)]
    res = pl.pallas_call(
        functools.partial(_delta_kernel, ctx_out=ctx_out),
        grid=(b, nh),
        in_specs=[small, small, cw(0), cw(nh), cw(2 * nh), pl.BlockSpec((1, dk), lambda i, j: (0, 0)),
                  col(t, 0), col(t, nh), col(t, 2 * nh), col(t, 0), gate(t),
                  col(l, 0), col(l, nh), col(l, 2 * nh), col(l, 0), gate(l)],
        out_specs=out_specs,
        out_shape=out_shape,
        scratch_shapes=[pltpu.VMEM((nmax + 2 * CONV_PAD, dk), F32),
                        pltpu.VMEM((nmax, dk), BF16), pltpu.VMEM((nmax, dk), BF16), pltpu.VMEM((nmax, dk), BF16),
                        pltpu.VMEM((nmax, dk), F32), pltpu.VMEM((nmax, dk), F32),
                        pltpu.VMEM((nmax // GROUP, 8, GROUP), F32),
                        pltpu.VMEM((N_MASKS, GROUP, GROUP), F32),
                        pltpu.VMEM((N_BMASKS, GROUP, GROUP), BF16)] + handover,
        compiler_params=_cparams(("arbitrary", "arbitrary")),
        name="delta_mixer",
    )(a_log_l.reshape(1, -1), dt_bias_l.reshape(1, -1), conv_w_l, conv_w_l, conv_w_l, norm_w_l.reshape(1, -1),
      aqkv, aqkv, aqkv, az, bat, aqkvc, aqkvc, aqkvc, azc, batc)
    return res if ctx_out else (res[0], None)


def _rope_angles(pos, n_freq):
    inv = ROPE_BASE ** (-jnp.arange(n_freq, dtype=F32) / n_freq)
    return pos[:, None] * inv[None, :]


def _position_tables(t):
    rows_n = t // GRID_W
    rows = jnp.repeat(jnp.arange(rows_n, dtype=F32), GRID_W)
    cols = jnp.tile(jnp.arange(GRID_W, dtype=F32), rows_n)
    n_ax = B_HD // 4
    ar, ac = _rope_angles(rows, n_ax), _rope_angles(cols, n_ax)
    cr, sr, cc, sc = jnp.cos(ar), jnp.sin(ar), jnp.cos(ac), jnp.sin(ac)
    zz = jnp.zeros_like(sr)
    reps = LANES // B_HD
    cosa = jnp.tile(jnp.concatenate([cr, cr, cc, cc], axis=1), (1, reps))
    sinm = jnp.tile(jnp.concatenate([-sr, zz, -sc, zz], axis=1), (1, reps))
    sinp = jnp.tile(jnp.concatenate([zz, sr, zz, sc], axis=1), (1, reps))
    at = _rope_angles(jnp.arange(t, dtype=F32), C_HD // 2)
    cosr = jnp.concatenate([jnp.cos(at), jnp.cos(at)], axis=1)
    sinr = jnp.concatenate([-jnp.sin(at), jnp.sin(at)], axis=1)
    return (cosa, sinm, sinp, cosr, sinr)


def _identity_tables(l):
    one, zero = jnp.ones((l, LANES), F32), jnp.zeros((l, LANES), F32)
    return (one, zero, zero, one, zero)


def _split_weights(w):
    sizes = (3 * A_WIDTH, A_WIDTH, 2 * A_HEADS, 2 * A_HEADS, B_Q_HEADS * B_HD, 2 * B_KV_HEADS * B_HD, BR_WIDTH,
             3 * BR_WIDTH, BR_WIDTH, N_BRANCH * D_MODEL)
    pts = np.cumsum(sizes)[:-1]
    aqkv, az, abeta, aalpha, bq, bkv, bz, cqkv, cz, mg = jnp.split(w, pts, axis=1)
    ba = jnp.concatenate([abeta, aalpha], axis=1)
    ba = jnp.pad(ba, ((0, 0), (0, LANES - ba.shape[1])))
    heads = [bkv[:, i * B_HD:(i + 1) * B_HD] for i in range(2 * B_KV_HEADS)]
    bkv2 = jnp.concatenate([hd for hd in heads for _ in range(LANES // B_HD)], axis=1)
    return [a.astype(BF16) for a in (aqkv, az, ba, bq, bkv2, bz, cqkv, cz, mg)]


def kernel(x, c, ctx, c_ctx, w_ada, b_ada, norm_w, w_in, a_conv_w, a_log, a_dt_bias, a_norm_w, b_sink, c_decay,
           c_norm_w, w_branch, w_out, final_norm_w):
    b, t, d = x.shape
    l = ctx.shape[1]
    depth = w_ada.shape[0]
    assert d == D_MODEL and t % INPROJ_TM == 0 and l % INPROJ_TM == 0
    assert t % MERGE_TM == 0 and (b * l) % MERGE_TM == 0 and t % GRID_W == 0

    n_mod = -(-(b + 1) // 8) * 8
    cc = jnp.concatenate([c, c_ctx[None, :], jnp.zeros((n_mod - b - 1, d), F32)], axis=0)
    mod_all = ada_mod(cc, w_ada, b_ada)
    x_tables = _position_tables(t)
    c_tables = _identity_tables(l)

    x2d = x.reshape(b * t, d)
    c2d = ctx.reshape(b * l, d)
    for layer in range(depth):
        last = layer == depth - 1
        mod = mod_all[layer].reshape(n_mod, 1, 3 * d)
        weights = _split_weights(w_in[layer])
        nw = norm_w[layer].reshape(1, d)
        px = in_proj(x2d, mod, lambda i: i // (t // INPROJ_TM), nw, x_tables, weights, t)
        pc = in_proj(c2d, mod, lambda i: b, nw, c_tables, weights, l)
        (aqkv, az, ba, bq, bk, bv, bz, cq, ck, cv, cz, mg) = [a.reshape(b, t, -1) for a in px]
        (aqkvc, azc, bac, bqc, bkc, bvc, bzc, cqc, ckc, cvc, czc, mgc) = [a.reshape(b, l, -1) for a in pc]
        bat = jnp.swapaxes(ba[:, :, :4 * A_HEADS], 1, 2)
        batc = jnp.swapaxes(bac[:, :, :4 * A_HEADS], 1, 2)

        ya, yac = delta_mixer(aqkv, az, bat, aqkvc, azc, batc, a_conv_w[layer], a_log[layer], a_dt_bias[layer],
                              a_norm_w[layer], not last)
        yb = window_attn(bq, bk, bv, bz, bkc, bvc, b_sink[layer])
        yc, ycc = retention(cq, ck, cv, cz, cqc, ckc, cvc, czc, c_decay[layer], c_norm_w[layer], not last)

        wbr = w_branch[layer].astype(BF16)
        wo = w_out[layer].astype(BF16)
        fw = final_norm_w.reshape(1, d)
        flat = lambda a: a.reshape(-1, a.shape[-1])
        x2d_new = merge_out(flat(ya), flat(yb), flat(yc), flat(mg), x2d, mod, lambda i: i // (t // MERGE_TM),
                            wbr, wo, fw, last)
        if not last:
            ybc = context_attn(bqc, bkc, bvc, bzc, b_sink[layer])
            c2d = merge_out(flat(yac), flat(ybc), flat(ycc), flat(mgc), c2d, mod, lambda i: b, wbr, wo, fw, False)
        x2d = x2d_new
    return x2d.reshape(b, t, d)
```

```python
import functools

import jax
import jax.numpy as jnp
import numpy as np
from jax import lax
from jax.experimental import pallas as pl
from jax.experimental.pallas import tpu as pltpu

F32 = jnp.float32
BF16 = jnp.bfloat16

D_MODEL = 1024
GRID_W = 64
EPS = 1e-6
ROPE_BASE = 10000.0
BR_WIDTH = D_MODEL // 2
A_DK = 128
A_HEADS = BR_WIDTH // A_DK
A_WIDTH = A_HEADS * A_DK
A_CONV = 5
A_CHUNK = 64
B_HD = 64
B_Q_HEADS = BR_WIDTH // B_HD
B_KV_HEADS = B_Q_HEADS // 4
B_BLOCK = 128
C_HD = 128
C_HEADS = BR_WIDTH // C_HD
N_BRANCH = 3

LANES = 128
GROUP = 256
VMEM_LIMIT = 56 * 1024 * 1024


def _cparams(sem):
    return pltpu.CompilerParams(dimension_semantics=sem, vmem_limit_bytes=VMEM_LIMIT)


def _const_spec(shape):
    nd = len(shape)
    return pl.BlockSpec(shape, lambda *_: (0,) * nd, pipeline_mode=pl.Buffered(1))


def _ada_kernel(c_ref, w_ref, b_ref, o_ref):
    c = c_ref[...]
    s = c * jax.nn.sigmoid(c)
    o_ref[...] = jnp.dot(s, w_ref[...], preferred_element_type=F32,
                         precision=lax.Precision.HIGHEST) + b_ref[...]


def ada_mod(cc, w_ada, b_ada):
    depth = w_ada.shape[0]
    r, d = cc.shape
    tn = 1024
    nt = (3 * d) // tn
    return pl.pallas_call(
        _ada_kernel,
        grid=(depth, nt),
        in_specs=[pl.BlockSpec((r, d), lambda l, j: (0, 0)),
                  pl.BlockSpec((None, d, tn), lambda l, j: (l, 0, j)),
                  pl.BlockSpec((None, 1, tn), lambda l, j: (l, 0, j))],
        out_specs=pl.BlockSpec((None, r, tn), lambda l, j: (l, 0, j)),
        out_shape=jax.ShapeDtypeStruct((depth, r, 3 * d), F32),
        compiler_params=_cparams(("arbitrary", "arbitrary")),
        name="ada_mod",
    )(cc, w_ada, b_ada.reshape(depth, 1, 3 * d))


def _rot_axial(x, cosa, sinm, sinp):
    w = x.shape[-1]
    reps = w // LANES
    ca = jnp.concatenate([cosa] * reps, axis=-1)
    sm = jnp.concatenate([sinm] * reps, axis=-1)
    sp = jnp.concatenate([sinp] * reps, axis=-1)
    return x * ca + pltpu.roll(x, w - 16, 1) * sm + pltpu.roll(x, 16, 1) * sp


def _rot_half128(x, cosr, sinr):
    outs = []
    for t in range(x.shape[-1] // LANES):
        xt = x[:, t * LANES:(t + 1) * LANES]
        outs.append(xt * cosr + pltpu.roll(xt, LANES // 2, 1) * sinr)
    return jnp.concatenate(outs, axis=-1)


def _inproj_kernel(x_ref, mod_ref, nw_ref, cosa_ref, sinm_ref, sinp_ref, cosr_ref, sinr_ref,
                   w_aqkv, w_az, w_ba, w_bq, w_bkv, w_bz, w_cqkv, w_cz, w_mg,
                   o_aqkv, o_az, o_ba, o_bq, o_bk, o_bv, o_bz, o_cq, o_ck, o_cv, o_cz, o_mg):
    d = D_MODEL
    x = x_ref[...]
    y = x * lax.rsqrt(jnp.mean(x * x, axis=-1, keepdims=True) + EPS) * nw_ref[...]
    mod = mod_ref[...]
    h = (y * (1.0 + mod[:, d:2 * d]) + mod[:, :d]).astype(BF16)

    def proj(w):
        return jnp.dot(h, w[...], preferred_element_type=F32)

    o_aqkv[...] = proj(w_aqkv)
    o_az[...] = proj(w_az)
    o_ba[...] = proj(w_ba)
    cosa, sinm, sinp = cosa_ref[...], sinm_ref[...], sinp_ref[...]
    o_bq[...] = (_rot_axial(proj(w_bq), cosa, sinm, sinp) * (B_HD ** -0.5 * LOG2E)).astype(BF16)
    bkv = proj(w_bkv)
    half = bkv.shape[-1] // 2
    o_bk[...] = _rot_axial(bkv[:, :half], cosa, sinm, sinp).astype(BF16)
    o_bv[...] = bkv[:, half:].astype(BF16)
    o_bz[...] = proj(w_bz)
    cqkv = proj(w_cqkv)
    cw = cqkv.shape[-1] // 3
    cosr, sinr = cosr_ref[...], sinr_ref[...]
    o_cq[...] = _rot_half128(cqkv[:, :cw], cosr, sinr).astype(BF16)
    o_ck[...] = (_rot_half128(cqkv[:, cw:2 * cw], cosr, sinr) * (C_HD ** -0.5)).astype(BF16)
    o_cv[...] = cqkv[:, 2 * cw:].astype(BF16)
    o_cz[...] = proj(w_cz)
    o_mg[...] = proj(w_mg).astype(BF16)


INPROJ_TM = 512


def _inproj_tile(seq_len):
    return min(INPROJ_TM, seq_len)


def in_proj(x2d, mod, mod_row_of_tile, nw, tables, weights, seq_len):
    r, d = x2d.shape
    tm = _inproj_tile(seq_len)
    assert seq_len % tm == 0
    tiles_per_seq = seq_len // tm
    row = lambda i: (i, 0)
    tab = lambda i: (i % tiles_per_seq, 0)
    in_specs = [pl.BlockSpec((tm, d), row),
                pl.BlockSpec((None, 1, 3 * d), lambda i: (mod_row_of_tile(i), 0, 0)),
                _const_spec((1, d))]
    in_specs += [pl.BlockSpec((tm, LANES), tab) for _ in range(5)]
    in_specs += [_const_spec(w.shape) for w in weights]
    widths = [(w.shape[1], F32) for w in weights]
    (aqkv, az, ba, bq, bkv, bz, cqkv, cz, mg) = [w.shape[1] for w in weights]
    outs = [(aqkv, F32), (az, F32), (ba, F32), (bq, BF16), (bkv // 2, BF16), (bkv // 2, BF16), (bz, F32),
            (cqkv // 3, BF16), (cqkv // 3, BF16), (cqkv // 3, BF16), (cz, F32), (mg, BF16)]
    del widths
    return pl.pallas_call(
        _inproj_kernel,
        grid=(r // tm,),
        in_specs=in_specs,
        out_specs=[pl.BlockSpec((tm, n), row) for n, _ in outs],
        out_shape=[jax.ShapeDtypeStruct((r, n), dt) for n, dt in outs],
        compiler_params=_cparams(("arbitrary",)),
        name="in_proj",
    )(x2d, mod, nw, *tables, *weights)


def _merge_kernel(ya_ref, yb_ref, yc_ref, mg_ref, x_ref, mod_ref, wbr_ref, wo_ref, fw_ref, o_ref, *, final):
    d = D_MODEL
    merged = None
    for i, y_ref in enumerate((ya_ref, yb_ref, yc_ref)):
        gate_i = jax.nn.sigmoid(mg_ref[:, i * d:(i + 1) * d].astype(F32))
        term = gate_i * jnp.dot(y_ref[...], wbr_ref[i], preferred_element_type=F32)
        merged = term if merged is None else merged + term
    out = jnp.dot(merged.astype(BF16), wo_ref[...], preferred_element_type=F32)
    xn = x_ref[...] + mod_ref[:, 2 * d:] * out
    if final:
        xn = xn * lax.rsqrt(jnp.mean(xn * xn, axis=-1, keepdims=True) + EPS) * fw_ref[...]
    o_ref[...] = xn


MERGE_TM = 512


def merge_out(ya, yb, yc, mg, x2d, mod, mod_row_of_tile, wbr, wo, fw, final):
    r, d = x2d.shape
    tm = MERGE_TM
    row = lambda i: (i, 0)
    bw = ya.shape[1]
    return pl.pallas_call(
        functools.partial(_merge_kernel, final=final),
        grid=(r // tm,),
        in_specs=[pl.BlockSpec((tm, bw), row), pl.BlockSpec((tm, bw), row), pl.BlockSpec((tm, bw), row),
                  pl.BlockSpec((tm, N_BRANCH * d), row), pl.BlockSpec((tm, d), row),
                  pl.BlockSpec((None, 1, 3 * d), lambda i: (mod_row_of_tile(i), 0, 0)),
                  _const_spec(wbr.shape), _const_spec(wo.shape), _const_spec((1, d))],
        out_specs=pl.BlockSpec((tm, d), row),
        out_shape=jax.ShapeDtypeStruct((r, d), F32),
        compiler_params=_cparams(("arbitrary",)),
        name="merge_out",
    )(ya, yb, yc, mg, x2d, mod, wbr, wo, fw)


def _dot_nt(a, b):
    return lax.dot_general(a, b, (((1,), (1,)), ((), ())), preferred_element_type=F32)


def _dot_tn(a, b):
    return lax.dot_general(a, b, (((0,), (0,)), ((), ())), preferred_element_type=F32)


def _run_interleaved(chains):
    active = list(chains)
    while active:
        for ch in list(active):
            try:
                next(ch)
            except StopIteration:
                active.remove(ch)


RET_GROUPS_PER_STEP = 4


def _retention_kernel(cd_ref, q_ref, k_ref, v_ref, z_ref, qc_ref, kc_ref, vc_ref, zc_ref, nw_ref,
                      *rest, ctx_out):
    if ctx_out:
        y_ref, yc_ref, u_scr, rhs_scr = rest
    else:
        (y_ref, u_scr, rhs_scr), yc_ref = rest, None
    h = pl.program_id(1)
    c = GROUP
    dh = q_ref.shape[-1]
    cd = cd_ref[...]
    lgv = jax.nn.log_sigmoid(cd)
    lane = lax.broadcasted_iota(jnp.int32, cd.shape, 1)
    lgf = jnp.sum(jnp.where(lane == h, lgv, 0.0), axis=1, keepdims=True)
    lgb = jnp.sum(jnp.where(lane == h + C_HEADS, lgv, 0.0), axis=1, keepdims=True)
    dij = (lax.broadcasted_iota(jnp.int32, (c, c), 0) - lax.broadcasted_iota(jnp.int32, (c, c), 1)).astype(F32)
    dmask = (jnp.where(dij >= 0, jnp.exp(jnp.maximum(dij, 0.0) * lgf), 0.0)
             + jnp.where(dij <= 0, jnp.exp(jnp.maximum(-dij, 0.0) * lgb), 0.0))
    rr = lax.broadcasted_iota(jnp.int32, (c, dh), 0).astype(F32)
    qdf = jnp.exp((rr + 1.0) * lgf)
    qdb = jnp.exp((c - rr) * lgb)
    kdf = jnp.exp((c - 1.0 - rr) * lgf)
    kdb = jnp.exp(rr * lgb)
    cdf = jnp.exp(c * lgf)
    cdb = jnp.exp(c * lgb)
    nw = nw_ref[...]

    def grouped_loop(ng, make_chain):
        per = RET_GROUPS_PER_STEP if ng % RET_GROUPS_PER_STEP == 0 else 1

        def body(i, _):
            _run_interleaved([make_chain(i * per + j) for j in range(per)])
            return 0
        lax.fori_loop(0, ng // per, body, 0)

    def state_pass(kr, vr, ng, sf, sb):
        def update_chain(g):
            r0 = pl.multiple_of(g * c, c)
            kg = kr[pl.ds(r0, c), :].astype(F32)
            lhs = jnp.concatenate([(kg * kdf).astype(BF16), (kg * kdb).astype(BF16)], axis=1)
            u = _dot_tn(lhs, vr[pl.ds(r0, c), :])
            yield
            u_scr[g] = u
        grouped_loop(ng, update_chain)

        def fbody(g, s):
            rhs_scr[g, 0:dh, :] = s.astype(BF16)
            return s * cdf + u_scr[g, 0:dh, :]
        sf = lax.fori_loop(0, ng, fbody, sf)

        def bbody(t, s):
            g = ng - 1 - t
            rhs_scr[g, dh:2 * dh, :] = s.astype(BF16)
            return s * cdb + u_scr[g, dh:2 * dh, :]
        sb = lax.fori_loop(0, ng, bbody, sb)
        return sf, sb

    def output_pass(qr, kr, vr, zr, yr, ng):
        def out_chain(g):
            r0 = pl.multiple_of(g * c, c)
            qg = qr[pl.ds(r0, c), :]
            sc = _dot_nt(qg, kr[pl.ds(r0, c), :])
            yield
            qf = qg.astype(F32)
            lhs = jnp.concatenate([(sc * dmask).astype(BF16), (qf * qdf).astype(BF16), (qf * qdb).astype(BF16)],
                                  axis=1)
            rhs = jnp.concatenate([vr[pl.ds(r0, c), :], rhs_scr[g]], axis=0)
            o = jnp.dot(lhs, rhs, preferred_element_type=F32)
            yield
            mu = jnp.mean(o, axis=-1, keepdims=True)
            oc = o - mu
            var = jnp.mean(oc * oc, axis=-1, keepdims=True)
            zg = zr[pl.ds(r0, c), :].astype(F32)
            yr[pl.ds(r0, c), :] = (oc * lax.rsqrt(var + EPS) * nw * (zg * jax.nn.sigmoid(zg))).astype(yr.dtype)
        grouped_loop(ng, out_chain)

    zeros = jnp.zeros((dh, dh), F32)
    n_c = qc_ref.shape[0] // c
    n_x = q_ref.shape[0] // c
    s_cf, s_cb = state_pass(kc_ref, vc_ref, n_c, zeros, zeros)
    if ctx_out:
        output_pass(qc_ref, kc_ref, vc_ref, zc_ref, yc_ref, n_c)
    state_pass(k_ref, v_ref, n_x, s_cf, s_cb)
    output_pass(q_ref, k_ref, v_ref, z_ref, y_ref, n_x)


def retention(cq, ck, cv, cz, cqc, ckc, cvc, czc, c_decay_l, c_norm_w_l, ctx_out):
    b, t, w = cq.shape
    l = cqc.shape[1]
    assert t % GROUP == 0 and l % GROUP == 0
    dh = C_HD
    seq = lambda n: pl.BlockSpec((None, n, dh), lambda i, j: (i, 0, j))
    out_shape = [jax.ShapeDtypeStruct((b, t, w), BF16)]
    out_specs = [seq(t)]
    if ctx_out:
        out_shape.append(jax.ShapeDtypeStruct((b, l, w), BF16))
        out_specs.append(seq(l))
    res = pl.pallas_call(
        functools.partial(_retention_kernel, ctx_out=ctx_out),
        grid=(b, C_HEADS),
        in_specs=[pl.BlockSpec((1, 2 * C_HEADS), lambda i, j: (0, 0)),
                  seq(t), seq(t), seq(t), seq(t), seq(l), seq(l), seq(l), seq(l),
                  pl.BlockSpec((1, dh), lambda i, j: (0, j))],
        out_specs=out_specs,
        out_shape=out_shape,
        scratch_shapes=[pltpu.VMEM((max(t, l) // GROUP, 2 * dh, dh), F32),
                        pltpu.VMEM((max(t, l) // GROUP, 2 * dh, dh), BF16)],
        compiler_params=_cparams(("arbitrary", "arbitrary")),
        name="retention",
    )(c_decay_l.reshape(1, -1), cq, ck, cv, cz, cqc, ckc, cvc, czc, c_norm_w_l.reshape(1, -1))
    return res if ctx_out else (res[0], None)


LOG2E = 1.4426950408889634


def _softmax_pv_chain(s, sink, vals, outs, key):
    m = jnp.maximum(jnp.max(s, axis=-1, keepdims=True), sink)
    yield
    p = jnp.exp2(s - m)
    den = jnp.sum(p, axis=-1, keepdims=True) + jnp.exp2(sink - m)
    o = jnp.dot(p.astype(BF16), vals, preferred_element_type=F32)
    yield
    outs[key] = o * pl.reciprocal(den)


def _head_lhs(q, g, lo, zero):
    grp = B_Q_HEADS // B_KV_HEADS
    rows = []
    for j in range(grp):
        t = (g * grp + j) // 2
        tile = q[:, t * LANES:(t + 1) * LANES]
        rows.append(jnp.where(lo, tile, zero) if j % 2 == 0 else jnp.where(lo, zero, tile))
    return jnp.concatenate(rows, axis=0)


WATTN_BLOCKS_PER_STEP = 2


def _wattn_kernel(sink_ref, q_ref, kp_ref, kc_ref, kn_ref, vp_ref, vc_ref, vn_ref, kx_ref, vx_ref, z_ref, y_ref):
    step = pl.program_id(1)
    nsteps = pl.num_programs(1)
    blk = B_BLOCK
    qb = WATTN_BLOCKS_PER_STEP
    l = kx_ref.shape[0]
    grp = B_Q_HEADS // B_KV_HEADS
    lo = lax.broadcasted_iota(jnp.int32, (blk, LANES), 1) < B_HD
    ri = lax.broadcasted_iota(jnp.int32, (blk, blk), 0)
    ci = lax.broadcasted_iota(jnp.int32, (blk, blk), 1)
    zero = jnp.zeros((blk, LANES), q_ref.dtype)
    kblocks = [kp_ref] + [kc_ref.at[a * blk:(a + 1) * blk] for a in range(qb)] + [kn_ref]
    vblocks = [vp_ref] + [vc_ref.at[a * blk:(a + 1) * blk] for a in range(qb)] + [vn_ref]
    for a in range(qb):
        rows = slice(a * blk, (a + 1) * blk)
        q = q_ref[rows, :]
        ok_prev = ci >= ri if a > 0 else jnp.logical_and(ci >= ri, step > 0)
        ok_next = ci <= ri if a < qb - 1 else jnp.logical_and(ci <= ri, step < nsteps - 1)
        outs = {}
        chains = []
        for g in range(B_KV_HEADS):
            gs = slice(g * LANES, (g + 1) * LANES)
            keys = jnp.concatenate([kx_ref[:, gs]] + [kblocks[a + i][:, gs] for i in range(3)], axis=0)
            vals = jnp.concatenate([vx_ref[:, gs]] + [vblocks[a + i][:, gs] for i in range(3)], axis=0)
            s_all = _dot_nt(_head_lhs(q, g, lo, zero), keys)
            for j in range(grp):
                s = s_all[j * blk:(j + 1) * blk]
                s = jnp.concatenate([s[:, :l],
                                     jnp.where(ok_prev, s[:, l:l + blk], -jnp.inf),
                                     s[:, l + blk:l + 2 * blk],
                                     jnp.where(ok_next, s[:, l + 2 * blk:], -jnp.inf)], axis=1)
                sink = jnp.full((blk, 1), sink_ref[0, g * grp + j] * LOG2E, F32)
                chains.append(_softmax_pv_chain(s, sink, vals, outs, g * grp + j))
        _run_interleaved(chains)
        tiles = [jnp.where(lo, outs[hd], outs[hd + 1]) for hd in range(0, B_Q_HEADS, 2)]
        z = z_ref[rows, :]
        y_ref[rows, :] = (jnp.concatenate(tiles, axis=1) * (z * jax.nn.sigmoid(z))).astype(y_ref.dtype)


def window_attn(bq, bk2, bv2, bz, bk2c, bv2c, sink_l):
    b, t, w = bq.shape
    l = bk2c.shape[1]
    blk = B_BLOCK
    nb = t // blk
    kw = bk2.shape[2]
    qb = WATTN_BLOCKS_PER_STEP
    assert nb % qb == 0
    prev = pl.BlockSpec((None, blk, kw), lambda i, n: (i, jnp.maximum(qb * n - 1, 0), 0))
    cur = pl.BlockSpec((None, qb * blk, kw), lambda i, n: (i, n, 0))
    nxt = pl.BlockSpec((None, blk, kw), lambda i, n: (i, jnp.minimum(qb * n + qb, nb - 1), 0))
    ctx = pl.BlockSpec((None, l, kw), lambda i, n: (i, 0, 0))
    qz = pl.BlockSpec((None, qb * blk, w), lambda i, n: (i, n, 0))
    return pl.pallas_call(
        _wattn_kernel,
        grid=(b, nb // qb),
        in_specs=[pl.BlockSpec(memory_space=pltpu.SMEM), qz, prev, cur, nxt, prev, cur, nxt, ctx, ctx, qz],
        out_specs=qz,
        out_shape=jax.ShapeDtypeStruct((b, t, w), BF16),
        compiler_params=_cparams(("arbitrary", "arbitrary")),
        name="window_attn",
    )(sink_l.reshape(1, -1), bq, bk2, bk2, bk2, bv2, bv2, bv2, bk2c, bv2c, bz)


def _cattn_kernel(sink_ref, q_ref, kx_ref, vx_ref, z_ref, y_ref):
    l = q_ref.shape[0]
    grp = B_Q_HEADS // B_KV_HEADS
    q = q_ref[...]
    lo = lax.broadcasted_iota(jnp.int32, (l, LANES), 1) < B_HD
    zero = jnp.zeros((l, LANES), q.dtype)
    outs = {}
    chains = []
    for g in range(B_KV_HEADS):
        gs = slice(g * LANES, (g + 1) * LANES)
        keys, vals = kx_ref[:, gs], vx_ref[:, gs]
        s_all = _dot_nt(_head_lhs(q, g, lo, zero), keys)
        for j in range(grp):
            sink = jnp.full((l, 1), sink_ref[0, g * grp + j] * LOG2E, F32)
            chains.append(_softmax_pv_chain(s_all[j * l:(j + 1) * l], sink, vals, outs, g * grp + j))
    _run_interleaved(chains)
    tiles = [jnp.where(lo, outs[hd], outs[hd + 1]) for hd in range(0, B_Q_HEADS, 2)]
    z = z_ref[...]
    y_ref[...] = (jnp.concatenate(tiles, axis=1) * (z * jax.nn.sigmoid(z))).astype(y_ref.dtype)


def context_attn(bqc, bk2c, bv2c, bzc, sink_l):
    b, l, w = bqc.shape
    kw = bk2c.shape[2]
    kv = pl.BlockSpec((None, l, kw), lambda i: (i, 0, 0))
    qz = pl.BlockSpec((None, l, w), lambda i: (i, 0, 0))
    return pl.pallas_call(
        _cattn_kernel,
        grid=(b,),
        in_specs=[pl.BlockSpec(memory_space=pltpu.SMEM), qz, kv, kv, qz],
        out_specs=qz,
        out_shape=jax.ShapeDtypeStruct((b, l, w), BF16),
        compiler_params=_cparams(("arbitrary",)),
        name="context_attn",
    )(sink_l.reshape(1, -1), bqc, bk2c, bv2c, bzc)


CONV_PAD = 8
GROUPS_PER_STEP = 2
CONV_TILES_PER_STEP = 4
_R_GCF, _R_BF, _R_GCB, _R_BB, _R_TEF, _R_TEB = range(6)
INV_BASE = 8
N_MERGE = int(np.log2(A_CHUNK // INV_BASE))
_M_NEGF, _M_NEGB, _M_STRICTF, _M_STRICTB, _M_SBS = range(5)
N_MASKS = 5
_B_SAME, _B_SAMEB, _B_OFF0 = range(3)
N_BMASKS = _B_OFF0 + N_MERGE


def _delta_kernel(alog_ref, dtb_ref, cwq_ref, cwk_ref, cwv_ref, nw_ref,
                  xq_ref, xk_ref, xv_ref, z_ref, bat_ref,
                  xqc_ref, xkc_ref, xvc_ref, zc_ref, batc_ref,
                  *rest, ctx_out):
    if ctx_out:
        y_ref, yc_ref = rest[:2]
        rest = rest[2:]
    else:
        y_ref, yc_ref = rest[0], None
        rest = rest[1:]
    xp, qn, kn, vn, of, ob, gates, masks, bmasks, p_buf, n_buf, qe_buf, ol_buf, tot_buf = rest
    h = pl.program_id(1)
    gsz = GROUP
    ck = A_CHUNK
    nck = gsz // ck
    dk = A_DK

    @pl.when(jnp.logical_and(pl.program_id(0) == 0, h == 0))
    def _build_masks():
        ri = lax.broadcasted_iota(jnp.int32, (gsz, gsz), 0)
        ci = lax.broadcasted_iota(jnp.int32, (gsz, gsz), 1)
        same = (ri // ck) == (ci // ck)
        masks[_M_NEGF] = jnp.where(jnp.logical_and(same, ri >= ci), 0.0, -jnp.inf)
        masks[_M_NEGB] = jnp.where(jnp.logical_and(same, ri <= ci), 0.0, -jnp.inf)
        masks[_M_STRICTF] = jnp.where(jnp.logical_and(same, ri > ci), 1.0, 0.0)
        masks[_M_STRICTB] = jnp.where(jnp.logical_and(same, ri < ci), 1.0, 0.0)
        cj = ci % ck
        masks[_M_SBS] = jnp.where(ri < ck, jnp.where(ri == cj, 1.0, 0.0),
                                  jnp.where(((ri - ck) // INV_BASE) == (cj // INV_BASE), 1.0, 0.0))
        bmasks[_B_SAME] = jnp.where(same, 1.0, 0.0).astype(BF16)
        bmasks[_B_SAMEB] = jnp.where((ri // INV_BASE) == (ci // INV_BASE), 1.0, 0.0).astype(BF16)
        for lvl in range(N_MERGE):
            m = INV_BASE << lvl
            bmasks[_B_OFF0 + lvl] = jnp.where(
                jnp.logical_and((ri // (2 * m)) == (ci // (2 * m)), (ri // m) != (ci // m)), 1.0, 0.0).astype(BF16)

    lane8 = lax.broadcasted_iota(jnp.int32, alog_ref.shape, 1)
    a_all = jnp.exp(alog_ref[...])
    dt_all = dtb_ref[...]

    def pick(v, idx):
        return jnp.sum(jnp.where(lane8 == idx, v, 0.0), axis=1, keepdims=True)

    a_f, a_b = pick(a_all, h), pick(a_all, h + A_HEADS)
    dt_f, dt_b = pick(dt_all, h), pick(dt_all, h + A_HEADS)
    nw = nw_ref[...]

    def conv_pass(x_ref, cw_ref, dst, n, l2_scale):
        zpad = jnp.zeros((CONV_PAD, dk), F32)
        xp[0:CONV_PAD, :] = zpad
        xp[CONV_PAD + n:2 * CONV_PAD + n, :] = zpad
        ng = n // gsz

        def cp(i, _):
            r0 = pl.multiple_of(i * gsz, gsz)
            xp[pl.ds(CONV_PAD + r0, gsz), :] = x_ref[pl.ds(r0, gsz), :]
            return 0
        lax.fori_loop(0, ng, cp, 0)
        w = cw_ref[...]

        def tile_chain(g):
            r0 = pl.multiple_of(g * gsz, gsz)
            acc = None
            for j in range(A_CONV):
                term = xp[pl.ds(r0 + (CONV_PAD - A_CONV // 2 + j), gsz), :] * w[j:j + 1, :]
                acc = term if acc is None else acc + term
            s = acc * jax.nn.sigmoid(acc)
            if l2_scale is not None:
                ss = jnp.sum(s * s, axis=-1, keepdims=True)
                yield
                s = s * (lax.rsqrt(ss + EPS) * l2_scale)
            dst[pl.ds(r0, gsz), :] = s.astype(BF16)

        per = CONV_TILES_PER_STEP if ng % CONV_TILES_PER_STEP == 0 else 1

        def body(i, _):
            _run_interleaved([tile_chain(i * per + j) for j in range(per)])
            return 0
        lax.fori_loop(0, ng // per, body, 0)

    def prologue(xq, xk, xv, bat, n):
        ng = n // gsz
        conv_pass(xq, cwq_ref, qn, n, dk ** -0.5)
        conv_pass(xk, cwk_ref, kn, n, 1.0)
        conv_pass(xv, cwv_ref, vn, n, None)

        beta_f = jax.nn.sigmoid(bat[pl.ds(h, 1), :])
        beta_b = jax.nn.sigmoid(bat[pl.ds(h + A_HEADS, 1), :])
        g_f = -a_f * jax.nn.softplus(bat[pl.ds(h + 2 * A_HEADS, 1), :] + dt_f)
        g_b = -a_b * jax.nn.softplus(bat[pl.ds(h + 3 * A_HEADS, 1), :] + dt_b)
        row = lax.broadcasted_iota(jnp.int32, (8, n), 0)
        pos = lax.broadcasted_iota(jnp.int32, (8, n), 1) % ck
        g2 = jnp.where(row == 0, g_f, jnp.where(row == 1, g_b, 0.0))
        cf, cb = g2, g2
        s = 1
        while s < ck:
            cf = cf + jnp.where(pos >= s, pltpu.roll(cf, s, 1), 0.0)
            cb = cb + jnp.where(pos < ck - s, pltpu.roll(cb, n - s, 1), 0.0)
            s *= 2
        tab = jnp.where(row == _R_GCF, cf[0:1], 0.0)
        tab = jnp.where(row == _R_BF, beta_f, tab)
        tab = jnp.where(row == _R_GCB, cb[1:2], tab)
        tab = jnp.where(row == _R_BB, beta_b, tab)
        tab = jnp.where(row == _R_TEF, cb[0:1] - g_f, tab)
        tab = jnp.where(row == _R_TEB, cf[1:2] - g_b, tab)
        for g in range(ng):
            gates[g] = tab[:, g * gsz:(g + 1) * gsz]

    def block_diag(m_sbs):
        return jnp.concatenate([m_sbs.astype(BF16)] * nck, axis=0) * bmasks[_B_SAME]

    def local_chain(g, slot, j, d):
        fwd = d == 0
        r0 = pl.multiple_of(g * gsz, gsz)
        kg = kn[pl.ds(r0, gsz), :]
        qg = qn[pl.ds(r0, gsz), :]
        vg = vn[pl.ds(r0, gsz), :]
        rg = gates[g]
        rgt = rg.T
        i_gc, i_b, i_te = (_R_GCF, _R_BF, _R_TEF) if fwd else (_R_GCB, _R_BB, _R_TEB)
        gc_row, b_row, te_row = rg[i_gc:i_gc + 1], rg[i_b:i_b + 1], rg[i_te:i_te + 1]
        gc_col, b_col, te_col = rgt[:, i_gc:i_gc + 1], rgt[:, i_b:i_b + 1], rgt[:, i_te:i_te + 1]
        dec = jnp.exp((gc_col - gc_row) + masks[_M_NEGF if fwd else _M_NEGB])
        ab = _dot_nt(jnp.concatenate([kg, qg], axis=0), kg)
        yield
        lm = ab[:gsz] * b_col * dec * masks[_M_STRICTF if fwd else _M_STRICTB]
        qm = (ab[gsz:] * dec).astype(BF16)
        lmb = lm.astype(BF16)
        l_sbs = lm[0:ck]
        for c in range(1, nck):
            l_sbs = l_sbs + lm[c * ck:(c + 1) * ck]
        ld_sbs = l_sbs * masks[_M_SBS, ck:2 * ck, :]
        x = masks[_M_SBS, 0:ck, :] - ld_sbs
        p = jnp.dot(ld_sbs.astype(BF16), lmb * bmasks[_B_SAMEB], preferred_element_type=F32)
        yield
        order = 2
        while order < INV_BASE:
            pbd = block_diag(p)
            if 2 * order < INV_BASE:
                r = jnp.dot(jnp.concatenate([x.astype(BF16), p.astype(BF16)], axis=0), pbd,
                            preferred_element_type=F32)
                yield
                x = x + r[:ck]
                p = r[ck:]
            else:
                r = jnp.dot(x.astype(BF16), pbd, preferred_element_type=F32)
                yield
                x = x + r
            order *= 2
        for lvl in range(N_MERGE):
            y = jnp.dot(x.astype(BF16), lmb * bmasks[_B_OFF0 + lvl], preferred_element_type=F32)
            yield
            r = jnp.dot(y.astype(BF16), block_diag(x), preferred_element_type=F32)
            yield
            x = x - r
        kgc = (kg.astype(F32) * jnp.exp(gc_col)).astype(BF16)
        wu = jnp.dot(block_diag(x * b_row), jnp.concatenate([kgc, vg], axis=1), preferred_element_type=F32)
        yield
        wu = wu.astype(BF16)
        qwu = jnp.dot(qm, wu, preferred_element_type=F32)
        kt = (kg.astype(F32) * jnp.exp(te_col)).astype(BF16)
        pn = [_dot_tn(kt[c * ck:(c + 1) * ck], wu[c * ck:(c + 1) * ck]) for c in range(nck)]
        yield
        ol_buf[slot, d, j] = qwu[:, dk:]
        qe_buf[slot, d, j] = (qg.astype(F32) * jnp.exp(gc_col) - qwu[:, :dk]).astype(BF16)
        for c in range(nck):
            p_buf[slot, d, j, c] = pn[c][:, :dk].astype(BF16)
            n_buf[slot, d, j, c] = pn[c][:, dk:]
        tot_buf[slot, d, j, 0:1, :] = jnp.exp(gc_row + te_row)

    def rec_chain(groups_fb, slot, s_f, s_b, out):
        zblk = jnp.zeros((dk, dk), BF16)
        for j, (gf, gb) in enumerate(zip(*groups_fb)):
            rf = pl.multiple_of(gf * gsz, gsz)
            rb = pl.multiple_of(gb * gsz, gsz)
            tot_f = tot_buf[slot, 0, j, 0:1, :]
            tot_b = tot_buf[slot, 1, j, 0:1, :]
            for cf in range(nck):
                cb = nck - 1 - cf
                lhs = jnp.concatenate(
                    [jnp.concatenate([p_buf[slot, 0, j, cf], qe_buf[slot, 0, j, cf * ck:(cf + 1) * ck, :]], axis=0),
                     jnp.concatenate([p_buf[slot, 1, j, cb], qe_buf[slot, 1, j, cb * ck:(cb + 1) * ck, :]], axis=0)],
                    axis=1)
                rhs = jnp.concatenate([jnp.concatenate([s_f.astype(BF16), zblk], axis=1),
                                       jnp.concatenate([zblk, s_b.astype(BF16)], axis=1)], axis=0)
                ps = jnp.dot(lhs, rhs, preferred_element_type=F32)
                yield
                of[pl.ds(rf + cf * ck, ck), :] = ol_buf[slot, 0, j, cf * ck:(cf + 1) * ck, :] + ps[dk:, :dk]
                ob[pl.ds(rb + cb * ck, ck), :] = ol_buf[slot, 1, j, cb * ck:(cb + 1) * ck, :] + ps[dk:, dk:]
                s_f = s_f * tot_f[:, cf * ck:cf * ck + 1] - ps[:dk, :dk] + n_buf[slot, 0, j, cf]
                s_b = s_b * tot_b[:, cb * ck:cb * ck + 1] - ps[:dk, dk:] + n_buf[slot, 1, j, cb]
        out[0], out[1] = s_f, s_b

    def sweep_plan(n):
        ng = n // gsz
        gps = GROUPS_PER_STEP if ng % GROUPS_PER_STEP == 0 else 1
        groups_f = lambda i: [i * gps + j for j in range(gps)]
        groups_b = lambda i: [ng - 1 - (i * gps + j) for j in range(gps)]

        def locals_of(i, slot):
            return ([local_chain(g, slot, j, 0) for j, g in enumerate(groups_f(i))]
                    + [local_chain(g, slot, j, 1) for j, g in enumerate(groups_b(i))])

        def recs_of(i, slot, s_f, s_b, out):
            return [rec_chain((groups_f(i), groups_b(i)), slot, s_f, s_b, out)]
        return locals_of, recs_of, ng // gps

    def pipelined(plan, s_f, s_b):
        locals_of, recs_of, nsteps = plan

        def body(i, carry):
            out = [None, None]
            slot = i % 2
            _run_interleaved(locals_of(i, slot) + recs_of(i - 1, 1 - slot, carry[0], carry[1], out))
            return tuple(out)
        s_f, s_b = lax.fori_loop(1, nsteps, body, (s_f, s_b))
        out = [None, None]
        _run_interleaved(recs_of(nsteps - 1, (nsteps - 1) % 2, s_f, s_b, out))
        return tuple(out)

    def epilogue(zr, yr, n):
        ng = n // gsz

        def tile_chain(g):
            r0 = pl.multiple_of(g * gsz, gsz)
            o = of[pl.ds(r0, gsz), :] + ob[pl.ds(r0, gsz), :]
            ms = jnp.mean(o * o, axis=-1, keepdims=True)
            yield
            zg = zr[pl.ds(r0, gsz), :]
            yr[pl.ds(r0, gsz), :] = (o * lax.rsqrt(ms + EPS) * nw * (zg * jax.nn.sigmoid(zg))).astype(yr.dtype)

        per = CONV_TILES_PER_STEP if ng % CONV_TILES_PER_STEP == 0 else 1

        def body(i, _):
            _run_interleaved([tile_chain(i * per + j) for j in range(per)])
            return 0
        lax.fori_loop(0, ng // per, body, 0)

    n_c, n_x = xqc_ref.shape[0], xq_ref.shape[0]
    zeros = jnp.zeros((dk, dk), F32)
    plan_c, plan_x = sweep_plan(n_c), sweep_plan(n_x)
    prologue(xqc_ref, xkc_ref, xvc_ref, batc_ref, n_c)
    if plan_c[2] == 1:
        _run_interleaved(plan_c[0](0, 1))
        prologue(xq_ref, xk_ref, xv_ref, bat_ref, n_x)
        out = [None, None]
        _run_interleaved(plan_c[1](0, 1, zeros, zeros, out) + plan_x[0](0, 0))
        s_f, s_b = out
        if ctx_out:
            epilogue(zc_ref, yc_ref, n_c)
    else:
        _run_interleaved(plan_c[0](0, 0))
        s_f, s_b = pipelined(plan_c, zeros, zeros)
        if ctx_out:
            epilogue(zc_ref, yc_ref, n_c)
        prologue(xq_ref, xk_ref, xv_ref, bat_ref, n_x)
        _run_interleaved(plan_x[0](0, 0))
    pipelined(plan_x, s_f, s_b)
    epilogue(z_ref, y_ref, n_x)


def delta_mixer(aqkv, az, bat, aqkvc, azc, batc, conv_w_l, a_log_l, dt_bias_l, norm_w_l, ctx_out):
    b, t, _ = aqkv.shape
    l = aqkvc.shape[1]
    assert t % GROUP == 0 and l % GROUP == 0
    dk = A_DK
    nh = A_HEADS
    col = lambda n, off: pl.BlockSpec((None, n, dk), lambda i, j: (i, 0, j + off))
    cw = lambda off: pl.BlockSpec((A_CONV, dk), lambda i, j: (0, j + off))
    small = pl.BlockSpec((1, 2 * nh), lambda i, j: (0, 0))
    gate = lambda n: pl.BlockSpec((None, 4 * nh, n), lambda i, j: (i, 0, 0))
    out_shape = [jax.ShapeDtypeStruct((b, t, nh * dk), BF16)]
    out_specs = [col(t, 0)]
    if ctx_out:
        out_shape.append(jax.ShapeDtypeStruct((b, l, nh * dk), BF16))
        out_specs.append(col(l, 0))
    nmax = max(t, l)
    nck = GROUP // A_CHUNK
    lead = (2, 2, GROUPS_PER_STEP)
    handover = [pltpu.VMEM(lead + (nck, dk, dk), BF16), pltpu.VMEM(lead + (nck, dk, dk), F32),
                pltpu.VMEM(lead + (GROUP, dk), BF16), pltpu.VMEM(lead + (GROUP, dk), F32),
                pltpu.VMEM(lead + (8, GROUP), F32)]
    res = pl.pallas_call(
        functools.partial(_delta_kernel, ctx_out=ctx_out),
        grid=(b, nh),
        in_specs=[small, small, cw(0), cw(nh), cw(2 * nh), pl.BlockSpec((1, dk), lambda i, j: (0, 0)),
                  col(t, 0), col(t, nh), col(t, 2 * nh), col(t, 0), gate(t),
                  col(l, 0), col(l, nh), col(l, 2 * nh), col(l, 0), gate(l)],
        out_specs=out_specs,
        out_shape=out_shape,
        scratch_shapes=[pltpu.VMEM((nmax + 2 * CONV_PAD, dk), F32),
                        pltpu.VMEM((nmax, dk), BF16), pltpu.VMEM((nmax, dk), BF16), pltpu.VMEM((nmax, dk), BF16),
                        pltpu.VMEM((nmax, dk), F32), pltpu.VMEM((nmax, dk), F32),
                        pltpu.VMEM((nmax // GROUP, 8, GROUP), F32),
                        pltpu.VMEM((N_MASKS, GROUP, GROUP), F32),
                        pltpu.VMEM((N_BMASKS, GROUP, GROUP), BF16)] + handover,
        compiler_params=_cparams(("arbitrary", "arbitrary")),
        name="delta_mixer",
    )(a_log_l.reshape(1, -1), dt_bias_l.reshape(1, -1), conv_w_l, conv_w_l, conv_w_l, norm_w_l.reshape(1, -1),
      aqkv, aqkv, aqkv, az, bat, aqkvc, aqkvc, aqkvc, azc, batc)
    return res if ctx_out else (res[0], None)


def _rope_angles(pos, n_freq):
    inv = ROPE_BASE ** (-jnp.arange(n_freq, dtype=F32) / n_freq)
    return pos[:, None] * inv[None, :]


def _position_tables(t):
    rows_n = t // GRID_W
    rows = jnp.repeat(jnp.arange(rows_n, dtype=F32), GRID_W)
    cols = jnp.tile(jnp.arange(GRID_W, dtype=F32), rows_n)
    n_ax = B_HD // 4
    ar, ac = _rope_angles(rows, n_ax), _rope_angles(cols, n_ax)
    cr, sr, cc, sc = jnp.cos(ar), jnp.sin(ar), jnp.cos(ac), jnp.sin(ac)
    zz = jnp.zeros_like(sr)
    reps = LANES // B_HD
    cosa = jnp.tile(jnp.concatenate([cr, cr, cc, cc], axis=1), (1, reps))
    sinm = jnp.tile(jnp.concatenate([-sr, zz, -sc, zz], axis=1), (1, reps))
    sinp = jnp.tile(jnp.concatenate([zz, sr, zz, sc], axis=1), (1, reps))
    at = _rope_angles(jnp.arange(t, dtype=F32), C_HD // 2)
    cosr = jnp.concatenate([jnp.cos(at), jnp.cos(at)], axis=1)
    sinr = jnp.concatenate([-jnp.sin(at), jnp.sin(at)], axis=1)
    return (cosa, sinm, sinp, cosr, sinr)


def _identity_tables(l):
    one, zero = jnp.ones((l, LANES), F32), jnp.zeros((l, LANES), F32)
    return (one, zero, zero, one, zero)


def _split_weights(w):
    sizes = (3 * A_WIDTH, A_WIDTH, 2 * A_HEADS, 2 * A_HEADS, B_Q_HEADS * B_HD, 2 * B_KV_HEADS * B_HD, BR_WIDTH,
             3 * BR_WIDTH, BR_WIDTH, N_BRANCH * D_MODEL)
    pts = np.cumsum(sizes)[:-1]
    aqkv, az, abeta, aalpha, bq, bkv, bz, cqkv, cz, mg = jnp.split(w, pts, axis=1)
    ba = jnp.concatenate([abeta, aalpha], axis=1)
    ba = jnp.pad(ba, ((0, 0), (0, LANES - ba.shape[1])))
    heads = [bkv[:, i * B_HD:(i + 1) * B_HD] for i in range(2 * B_KV_HEADS)]
    bkv2 = jnp.concatenate([hd for hd in heads for _ in range(LANES // B_HD)], axis=1)
    return [a.astype(BF16) for a in (aqkv, az, ba, bq, bkv2, bz, cqkv, cz, mg)]


def kernel(x, c, ctx, c_ctx, w_ada, b_ada, norm_w, w_in, a_conv_w, a_log, a_dt_bias, a_norm_w, b_sink, c_decay,
           c_norm_w, w_branch, w_out, final_norm_w):
    b, t, d = x.shape
    l = ctx.shape[1]
    depth = w_ada.shape[0]
    assert d == D_MODEL
    assert t % MERGE_TM == 0 and (b * l) % MERGE_TM == 0 and t % GRID_W == 0

    n_mod = -(-(b + 1) // 8) * 8
    cc = jnp.concatenate([c, c_ctx[None, :], jnp.zeros((n_mod - b - 1, d), F32)], axis=0)
    mod_all = ada_mod(cc, w_ada, b_ada)
    x_tables = _position_tables(t)
    c_tables = _identity_tables(l)

    x2d = x.reshape(b * t, d)
    c2d = ctx.reshape(b * l, d)
    for layer in range(depth):
        last = layer == depth - 1
        mod = mod_all[layer].reshape(n_mod, 1, 3 * d)
        weights = _split_weights(w_in[layer])
        nw = norm_w[layer].reshape(1, d)
        px = in_proj(x2d, mod, lambda i: i // (t // _inproj_tile(t)), nw, x_tables, weights, t)
        pc = in_proj(c2d, mod, lambda i: b, nw, c_tables, weights, l)
        (aqkv, az, ba, bq, bk, bv, bz, cq, ck, cv, cz, mg) = [a.reshape(b, t, -1) for a in px]
        (aqkvc, azc, bac, bqc, bkc, bvc, bzc, cqc, ckc, cvc, czc, mgc) = [a.reshape(b, l, -1) for a in pc]
        bat = jnp.swapaxes(ba[:, :, :4 * A_HEADS], 1, 2)
        batc = jnp.swapaxes(bac[:, :, :4 * A_HEADS], 1, 2)

        ya, yac = delta_mixer(aqkv, az, bat, aqkvc, azc, batc, a_conv_w[layer], a_log[layer], a_dt_bias[layer],
                              a_norm_w[layer], not last)
        yb = window_attn(bq, bk, bv, bz, bkc, bvc, b_sink[layer])
        yc, ycc = retention(cq, ck, cv, cz, cqc, ckc, cvc, czc, c_decay[layer], c_norm_w[layer], not last)

        wbr = w_branch[layer].astype(BF16)
        wo = w_out[layer].astype(BF16)
        fw = final_norm_w.reshape(1, d)
        flat = lambda a: a.reshape(-1, a.shape[-1])
        x2d_new = merge_out(flat(ya), flat(yb), flat(yc), flat(mg), x2d, mod, lambda i: i // (t // MERGE_TM),
                            wbr, wo, fw, last)
        if not last:
            ybc = context_attn(bqc, bkc, bvc, bzc, b_sink[layer])
            c2d = merge_out(flat(yac), flat(ybc), flat(ycc), flat(mgc), c2d, mod, lambda i: b, wbr, wo, fw, False)
        x2d = x2d_new
    return x2d.reshape(b, t, d)
```

```python
import functools

import jax
import jax.numpy as jnp
import numpy as np
from jax import lax
from jax.experimental import pallas as pl
from jax.experimental.pallas import tpu as pltpu

F32 = jnp.float32
BF16 = jnp.bfloat16

D_MODEL = 1024
GRID_W = 64
EPS = 1e-6
ROPE_BASE = 10000.0
BR_WIDTH = D_MODEL // 2
A_DK = 128
A_HEADS = BR_WIDTH // A_DK
A_WIDTH = A_HEADS * A_DK
A_CONV = 5
A_CHUNK = 64
B_HD = 64
B_Q_HEADS = BR_WIDTH // B_HD
B_KV_HEADS = B_Q_HEADS // 4
B_BLOCK = 128
C_HD = 128
C_HEADS = BR_WIDTH // C_HD
N_BRANCH = 3

LANES = 128
GROUP = 256
VMEM_LIMIT = 56 * 1024 * 1024


def _cparams(sem):
    return pltpu.CompilerParams(dimension_semantics=sem, vmem_limit_bytes=VMEM_LIMIT)


def _const_spec(shape):
    nd = len(shape)
    return pl.BlockSpec(shape, lambda *_: (0,) * nd, pipeline_mode=pl.Buffered(1))


def _ada_kernel(c_ref, w_ref, b_ref, o_ref):
    c = c_ref[...]
    s = c * jax.nn.sigmoid(c)
    o_ref[...] = jnp.dot(s, w_ref[...], preferred_element_type=F32,
                         precision=lax.Precision.HIGHEST) + b_ref[...]


def ada_mod(cc, w_ada, b_ada):
    depth = w_ada.shape[0]
    r, d = cc.shape
    tn = 1024
    nt = (3 * d) // tn
    return pl.pallas_call(
        _ada_kernel,
        grid=(depth, nt),
        in_specs=[pl.BlockSpec((r, d), lambda l, j: (0, 0)),
                  pl.BlockSpec((None, d, tn), lambda l, j: (l, 0, j)),
                  pl.BlockSpec((None, 1, tn), lambda l, j: (l, 0, j))],
        out_specs=pl.BlockSpec((None, r, tn), lambda l, j: (l, 0, j)),
        out_shape=jax.ShapeDtypeStruct((depth, r, 3 * d), F32),
        compiler_params=_cparams(("arbitrary", "arbitrary")),
        name="ada_mod",
    )(cc, w_ada, b_ada.reshape(depth, 1, 3 * d))


def _rot_axial(x, cosa, sinm, sinp):
    w = x.shape[-1]
    reps = w // LANES
    ca = jnp.concatenate([cosa] * reps, axis=-1)
    sm = jnp.concatenate([sinm] * reps, axis=-1)
    sp = jnp.concatenate([sinp] * reps, axis=-1)
    return x * ca + pltpu.roll(x, w - 16, 1) * sm + pltpu.roll(x, 16, 1) * sp


def _rot_half128(x, cosr, sinr):
    outs = []
    for t in range(x.shape[-1] // LANES):
        xt = x[:, t * LANES:(t + 1) * LANES]
        outs.append(xt * cosr + pltpu.roll(xt, LANES // 2, 1) * sinr)
    return jnp.concatenate(outs, axis=-1)


LOG2E = 1.4426950408889634

W_SIZES = (("aqkv", 3 * A_WIDTH), ("az", A_WIDTH), ("ba", LANES), ("bq", B_Q_HEADS * B_HD),
           ("bkv", 2 * B_KV_HEADS * LANES), ("bz", BR_WIDTH), ("cqkv", 3 * BR_WIDTH), ("cz", BR_WIDTH),
           ("mg", N_BRANCH * D_MODEL))
W_OFF = {}
_off = 0
for _name, _n in W_SIZES:
    W_OFF[_name] = (_off, _off + _n)
    _off += _n
W_TOTAL = _off


def _inproj_kernel(x_ref, mod_ref, nw_ref, cosa_ref, sinm_ref, sinp_ref, cosr_ref, sinr_ref, w_ref,
                   o_aqkv, o_az, o_ba, o_bq, o_bk, o_bv, o_bz, o_cq, o_ck, o_cv, o_cz, o_mg):
    d = D_MODEL
    x = x_ref[...]
    y = x * lax.rsqrt(jnp.mean(x * x, axis=-1, keepdims=True) + EPS) * nw_ref[...]
    mod = mod_ref[...]
    h = (y * (1.0 + mod[:, d:2 * d]) + mod[:, :d]).astype(BF16)

    def proj(name):
        a, b = W_OFF[name]
        return jnp.dot(h, w_ref[:, a:b], preferred_element_type=F32)

    o_aqkv[...] = proj("aqkv")
    o_az[...] = proj("az")
    o_ba[...] = proj("ba")
    cosa, sinm, sinp = cosa_ref[...], sinm_ref[...], sinp_ref[...]
    o_bq[...] = (_rot_axial(proj("bq"), cosa, sinm, sinp) * (B_HD ** -0.5 * LOG2E)).astype(BF16)
    bkv = proj("bkv")
    half = bkv.shape[-1] // 2
    o_bk[...] = _rot_axial(bkv[:, :half], cosa, sinm, sinp).astype(BF16)
    o_bv[...] = bkv[:, half:].astype(BF16)
    o_bz[...] = proj("bz")
    cqkv = proj("cqkv")
    cw = cqkv.shape[-1] // 3
    cosr, sinr = cosr_ref[...], sinr_ref[...]
    o_cq[...] = _rot_half128(cqkv[:, :cw], cosr, sinr).astype(BF16)
    o_ck[...] = (_rot_half128(cqkv[:, cw:2 * cw], cosr, sinr) * (C_HD ** -0.5)).astype(BF16)
    o_cv[...] = cqkv[:, 2 * cw:].astype(BF16)
    o_cz[...] = proj("cz")
    o_mg[...] = proj("mg").astype(BF16)


INPROJ_TM = 512


def _inproj_tile(seq_len):
    return min(INPROJ_TM, seq_len)


def in_proj(x2d, mod, mod_row_of_tile, nw, tables, w_all, seq_len):
    r, d = x2d.shape
    tm = _inproj_tile(seq_len)
    assert seq_len % tm == 0
    tiles_per_seq = seq_len // tm
    row = lambda i: (i, 0)
    tab = lambda i: (i % tiles_per_seq, 0)
    in_specs = [pl.BlockSpec((tm, d), row),
                pl.BlockSpec((None, 1, 3 * d), lambda i: (mod_row_of_tile(i), 0, 0)),
                _const_spec((1, d))]
    in_specs += [pl.BlockSpec((tm, LANES), tab) for _ in range(5)]
    in_specs += [_const_spec(w_all.shape)]
    (aqkv, az, ba, bq, bkv, bz, cqkv, cz, mg) = [n for _, n in W_SIZES]
    outs = [(aqkv, F32), (az, F32), (ba, F32), (bq, BF16), (bkv // 2, BF16), (bkv // 2, BF16), (bz, F32),
            (cqkv // 3, BF16), (cqkv // 3, BF16), (cqkv // 3, BF16), (cz, F32), (mg, BF16)]
    return pl.pallas_call(
        _inproj_kernel,
        grid=(r // tm,),
        in_specs=in_specs,
        out_specs=[pl.BlockSpec((tm, n), row) for n, _ in outs],
        out_shape=[jax.ShapeDtypeStruct((r, n), dt) for n, dt in outs],
        compiler_params=_cparams(("arbitrary",)),
        name="in_proj",
    )(x2d, mod, nw, *tables, w_all)


def _merge_kernel(ya_ref, yb_ref, yc_ref, mg_ref, x_ref, mod_ref, wbr_ref, wo_ref, fw_ref, o_ref, *, final):
    d = D_MODEL
    merged = None
    for i, y_ref in enumerate((ya_ref, yb_ref, yc_ref)):
        gate_i = jax.nn.sigmoid(mg_ref[:, i * d:(i + 1) * d].astype(F32))
        term = gate_i * jnp.dot(y_ref[...], wbr_ref[i], preferred_element_type=F32)
        merged = term if merged is None else merged + term
    out = jnp.dot(merged.astype(BF16), wo_ref[...], preferred_element_type=F32)
    xn = x_ref[...] + mod_ref[:, 2 * d:] * out
    if final:
        xn = xn * lax.rsqrt(jnp.mean(xn * xn, axis=-1, keepdims=True) + EPS) * fw_ref[...]
    o_ref[...] = xn


MERGE_TM = 512


def merge_out(ya, yb, yc, mg, x2d, mod, mod_row_of_tile, wbr, wo, fw, final):
    r, d = x2d.shape
    tm = MERGE_TM
    row = lambda i: (i, 0)
    bw = ya.shape[1]
    return pl.pallas_call(
        functools.partial(_merge_kernel, final=final),
        grid=(r // tm,),
        in_specs=[pl.BlockSpec((tm, bw), row), pl.BlockSpec((tm, bw), row), pl.BlockSpec((tm, bw), row),
                  pl.BlockSpec((tm, N_BRANCH * d), row), pl.BlockSpec((tm, d), row),
                  pl.BlockSpec((None, 1, 3 * d), lambda i: (mod_row_of_tile(i), 0, 0)),
                  _const_spec(wbr.shape), _const_spec(wo.shape), _const_spec((1, d))],
        out_specs=pl.BlockSpec((tm, d), row),
        out_shape=jax.ShapeDtypeStruct((r, d), F32),
        compiler_params=_cparams(("arbitrary",)),
        name="merge_out",
    )(ya, yb, yc, mg, x2d, mod, wbr, wo, fw)


def _dot_nt(a, b):
    return lax.dot_general(a, b, (((1,), (1,)), ((), ())), preferred_element_type=F32)


def _dot_tn(a, b):
    return lax.dot_general(a, b, (((0,), (0,)), ((), ())), preferred_element_type=F32)


def _run_interleaved(chains):
    active = list(chains)
    while active:
        for ch in list(active):
            try:
                next(ch)
            except StopIteration:
                active.remove(ch)


RET_GROUPS_PER_STEP = 4


def _retention_kernel(cd_ref, q_ref, k_ref, v_ref, z_ref, qc_ref, kc_ref, vc_ref, zc_ref, nw_ref,
                      *rest, ctx_out):
    if ctx_out:
        y_ref, yc_ref, u_scr, rhs_scr = rest
    else:
        (y_ref, u_scr, rhs_scr), yc_ref = rest, None
    h = pl.program_id(1)
    c = GROUP
    dh = q_ref.shape[-1]
    cd = cd_ref[...]
    lgv = jax.nn.log_sigmoid(cd)
    lane = lax.broadcasted_iota(jnp.int32, cd.shape, 1)
    lgf = jnp.sum(jnp.where(lane == h, lgv, 0.0), axis=1, keepdims=True)
    lgb = jnp.sum(jnp.where(lane == h + C_HEADS, lgv, 0.0), axis=1, keepdims=True)
    dij = (lax.broadcasted_iota(jnp.int32, (c, c), 0) - lax.broadcasted_iota(jnp.int32, (c, c), 1)).astype(F32)
    dmask = (jnp.where(dij >= 0, jnp.exp(jnp.maximum(dij, 0.0) * lgf), 0.0)
             + jnp.where(dij <= 0, jnp.exp(jnp.maximum(-dij, 0.0) * lgb), 0.0))
    rr = lax.broadcasted_iota(jnp.int32, (c, dh), 0).astype(F32)
    qdf = jnp.exp((rr + 1.0) * lgf)
    qdb = jnp.exp((c - rr) * lgb)
    kdf = jnp.exp((c - 1.0 - rr) * lgf)
    kdb = jnp.exp(rr * lgb)
    cdf = jnp.exp(c * lgf)
    cdb = jnp.exp(c * lgb)
    nw = nw_ref[...]

    def grouped_loop(ng, make_chain):
        per = RET_GROUPS_PER_STEP if ng % RET_GROUPS_PER_STEP == 0 else 1

        def body(i, _):
            _run_interleaved([make_chain(i * per + j) for j in range(per)])
            return 0
        lax.fori_loop(0, ng // per, body, 0)

    def state_pass(kr, vr, ng, sf, sb):
        def update_chain(g):
            r0 = pl.multiple_of(g * c, c)
            kg = kr[pl.ds(r0, c), :].astype(F32)
            lhs = jnp.concatenate([(kg * kdf).astype(BF16), (kg * kdb).astype(BF16)], axis=1)
            u = _dot_tn(lhs, vr[pl.ds(r0, c), :])
            yield
            u_scr[g] = u
        grouped_loop(ng, update_chain)

        def fbody(g, s):
            rhs_scr[g, 0:dh, :] = s.astype(BF16)
            return s * cdf + u_scr[g, 0:dh, :]
        sf = lax.fori_loop(0, ng, fbody, sf)

        def bbody(t, s):
            g = ng - 1 - t
            rhs_scr[g, dh:2 * dh, :] = s.astype(BF16)
            return s * cdb + u_scr[g, dh:2 * dh, :]
        sb = lax.fori_loop(0, ng, bbody, sb)
        return sf, sb

    def output_pass(qr, kr, vr, zr, yr, ng):
        def out_chain(g):
            r0 = pl.multiple_of(g * c, c)
            qg = qr[pl.ds(r0, c), :]
            sc = _dot_nt(qg, kr[pl.ds(r0, c), :])
            yield
            qf = qg.astype(F32)
            lhs = jnp.concatenate([(sc * dmask).astype(BF16), (qf * qdf).astype(BF16), (qf * qdb).astype(BF16)],
                                  axis=1)
            rhs = jnp.concatenate([vr[pl.ds(r0, c), :], rhs_scr[g]], axis=0)
            o = jnp.dot(lhs, rhs, preferred_element_type=F32)
            yield
            mu = jnp.mean(o, axis=-1, keepdims=True)
            oc = o - mu
            var = jnp.mean(oc * oc, axis=-1, keepdims=True)
            zg = zr[pl.ds(r0, c), :].astype(F32)
            yr[pl.ds(r0, c), :] = (oc * lax.rsqrt(var + EPS) * nw * (zg * jax.nn.sigmoid(zg))).astype(yr.dtype)
        grouped_loop(ng, out_chain)

    zeros = jnp.zeros((dh, dh), F32)
    n_c = qc_ref.shape[0] // c
    n_x = q_ref.shape[0] // c
    s_cf, s_cb = state_pass(kc_ref, vc_ref, n_c, zeros, zeros)
    if ctx_out:
        output_pass(qc_ref, kc_ref, vc_ref, zc_ref, yc_ref, n_c)
    state_pass(k_ref, v_ref, n_x, s_cf, s_cb)
    output_pass(q_ref, k_ref, v_ref, z_ref, y_ref, n_x)


def retention(cq, ck, cv, cz, cqc, ckc, cvc, czc, c_decay_l, c_norm_w_l, ctx_out):
    b, t, w = cq.shape
    l = cqc.shape[1]
    assert t % GROUP == 0 and l % GROUP == 0
    dh = C_HD
    seq = lambda n: pl.BlockSpec((None, n, dh), lambda i, j: (i, 0, j))
    out_shape = [jax.ShapeDtypeStruct((b, t, w), BF16)]
    out_specs = [seq(t)]
    if ctx_out:
        out_shape.append(jax.ShapeDtypeStruct((b, l, w), BF16))
        out_specs.append(seq(l))
    res = pl.pallas_call(
        functools.partial(_retention_kernel, ctx_out=ctx_out),
        grid=(b, C_HEADS),
        in_specs=[pl.BlockSpec((1, 2 * C_HEADS), lambda i, j: (0, 0)),
                  seq(t), seq(t), seq(t), seq(t), seq(l), seq(l), seq(l), seq(l),
                  pl.BlockSpec((1, dh), lambda i, j: (0, j))],
        out_specs=out_specs,
        out_shape=out_shape,
        scratch_shapes=[pltpu.VMEM((max(t, l) // GROUP, 2 * dh, dh), F32),
                        pltpu.VMEM((max(t, l) // GROUP, 2 * dh, dh), BF16)],
        compiler_params=_cparams(("arbitrary", "arbitrary")),
        name="retention",
    )(c_decay_l.reshape(1, -1), cq, ck, cv, cz, cqc, ckc, cvc, czc, c_norm_w_l.reshape(1, -1))
    return res if ctx_out else (res[0], None)


def _softmax_pv_chain(s, sink, vals, outs, key):
    m = jnp.maximum(jnp.max(s, axis=-1, keepdims=True), sink)
    yield
    p = jnp.exp2(s - m)
    den = jnp.sum(p, axis=-1, keepdims=True) + jnp.exp2(sink - m)
    o = jnp.dot(p.astype(BF16), vals, preferred_element_type=F32)
    yield
    outs[key] = o * pl.reciprocal(den)


def _head_lhs(q, g, lo, zero):
    grp = B_Q_HEADS // B_KV_HEADS
    rows = []
    for j in range(grp):
        t = (g * grp + j) // 2
        tile = q[:, t * LANES:(t + 1) * LANES]
        rows.append(jnp.where(lo, tile, zero) if j % 2 == 0 else jnp.where(lo, zero, tile))
    return jnp.concatenate(rows, axis=0)


WATTN_BLOCKS_PER_STEP = 4


def _wattn_kernel(sink_ref, q_ref, kp_ref, kc_ref, kn_ref, vp_ref, vc_ref, vn_ref, kx_ref, vx_ref, z_ref, y_ref):
    step = pl.program_id(1)
    nsteps = pl.num_programs(1)
    blk = B_BLOCK
    qb = WATTN_BLOCKS_PER_STEP
    l = kx_ref.shape[0]
    grp = B_Q_HEADS // B_KV_HEADS
    lo = lax.broadcasted_iota(jnp.int32, (blk, LANES), 1) < B_HD
    ri = lax.broadcasted_iota(jnp.int32, (blk, blk), 0)
    ci = lax.broadcasted_iota(jnp.int32, (blk, blk), 1)
    zero = jnp.zeros((blk, LANES), q_ref.dtype)
    kblocks = [kp_ref] + [kc_ref.at[a * blk:(a + 1) * blk] for a in range(qb)] + [kn_ref]
    vblocks = [vp_ref] + [vc_ref.at[a * blk:(a + 1) * blk] for a in range(qb)] + [vn_ref]
    for a in range(qb):
        rows = slice(a * blk, (a + 1) * blk)
        q = q_ref[rows, :]
        ok_prev = ci >= ri if a > 0 else jnp.logical_and(ci >= ri, step > 0)
        ok_next = ci <= ri if a < qb - 1 else jnp.logical_and(ci <= ri, step < nsteps - 1)
        outs = {}
        chains = []
        for g in range(B_KV_HEADS):
            gs = slice(g * LANES, (g + 1) * LANES)
            keys = jnp.concatenate([kx_ref[:, gs]] + [kblocks[a + i][:, gs] for i in range(3)], axis=0)
            vals = jnp.concatenate([vx_ref[:, gs]] + [vblocks[a + i][:, gs] for i in range(3)], axis=0)
            s_all = _dot_nt(_head_lhs(q, g, lo, zero), keys)
            for j in range(grp):
                s = s_all[j * blk:(j + 1) * blk]
                s = jnp.concatenate([s[:, :l],
                                     jnp.where(ok_prev, s[:, l:l + blk], -jnp.inf),
                                     s[:, l + blk:l + 2 * blk],
                                     jnp.where(ok_next, s[:, l + 2 * blk:], -jnp.inf)], axis=1)
                sink = jnp.full((blk, 1), sink_ref[0, g * grp + j] * LOG2E, F32)
                chains.append(_softmax_pv_chain(s, sink, vals, outs, g * grp + j))
        _run_interleaved(chains)
        tiles = [jnp.where(lo, outs[hd], outs[hd + 1]) for hd in range(0, B_Q_HEADS, 2)]
        z = z_ref[rows, :]
        y_ref[rows, :] = (jnp.concatenate(tiles, axis=1) * (z * jax.nn.sigmoid(z))).astype(y_ref.dtype)


def window_attn(bq, bk2, bv2, bz, bk2c, bv2c, sink_l):
    b, t, w = bq.shape
    l = bk2c.shape[1]
    blk = B_BLOCK
    nb = t // blk
    kw = bk2.shape[2]
    qb = WATTN_BLOCKS_PER_STEP
    assert nb % qb == 0
    prev = pl.BlockSpec((None, blk, kw), lambda i, n: (i, jnp.maximum(qb * n - 1, 0), 0))
    cur = pl.BlockSpec((None, qb * blk, kw), lambda i, n: (i, n, 0))
    nxt = pl.BlockSpec((None, blk, kw), lambda i, n: (i, jnp.minimum(qb * n + qb, nb - 1), 0))
    ctx = pl.BlockSpec((None, l, kw), lambda i, n: (i, 0, 0))
    qz = pl.BlockSpec((None, qb * blk, w), lambda i, n: (i, n, 0))
    return pl.pallas_call(
        _wattn_kernel,
        grid=(b, nb // qb),
        in_specs=[pl.BlockSpec(memory_space=pltpu.SMEM), qz, prev, cur, nxt, prev, cur, nxt, ctx, ctx, qz],
        out_specs=qz,
        out_shape=jax.ShapeDtypeStruct((b, t, w), BF16),
        compiler_params=_cparams(("arbitrary", "arbitrary")),
        name="window_attn",
    )(sink_l.reshape(1, -1), bq, bk2, bk2, bk2, bv2, bv2, bv2, bk2c, bv2c, bz)


def _cattn_kernel(sink_ref, q_ref, kx_ref, vx_ref, z_ref, y_ref):
    l = q_ref.shape[0]
    grp = B_Q_HEADS // B_KV_HEADS
    q = q_ref[...]
    lo = lax.broadcasted_iota(jnp.int32, (l, LANES), 1) < B_HD
    zero = jnp.zeros((l, LANES), q.dtype)
    outs = {}
    chains = []
    for g in range(B_KV_HEADS):
        gs = slice(g * LANES, (g + 1) * LANES)
        keys, vals = kx_ref[:, gs], vx_ref[:, gs]
        s_all = _dot_nt(_head_lhs(q, g, lo, zero), keys)
        for j in range(grp):
            sink = jnp.full((l, 1), sink_ref[0, g * grp + j] * LOG2E, F32)
            chains.append(_softmax_pv_chain(s_all[j * l:(j + 1) * l], sink, vals, outs, g * grp + j))
    _run_interleaved(chains)
    tiles = [jnp.where(lo, outs[hd], outs[hd + 1]) for hd in range(0, B_Q_HEADS, 2)]
    z = z_ref[...]
    y_ref[...] = (jnp.concatenate(tiles, axis=1) * (z * jax.nn.sigmoid(z))).astype(y_ref.dtype)


def context_attn(bqc, bk2c, bv2c, bzc, sink_l):
    b, l, w = bqc.shape
    kw = bk2c.shape[2]
    kv = pl.BlockSpec((None, l, kw), lambda i: (i, 0, 0))
    qz = pl.BlockSpec((None, l, w), lambda i: (i, 0, 0))
    return pl.pallas_call(
        _cattn_kernel,
        grid=(b,),
        in_specs=[pl.BlockSpec(memory_space=pltpu.SMEM), qz, kv, kv, qz],
        out_specs=qz,
        out_shape=jax.ShapeDtypeStruct((b, l, w), BF16),
        compiler_params=_cparams(("arbitrary",)),
        name="context_attn",
    )(sink_l.reshape(1, -1), bqc, bk2c, bv2c, bzc)


CONV_PAD = 8
GROUPS_PER_STEP = 4
CONV_TILES_PER_STEP = 4
_R_GCF, _R_BF, _R_GCB, _R_BB, _R_TEF, _R_TEB = range(6)
INV_BASE = 8
N_MERGE = int(np.log2(A_CHUNK // INV_BASE))
_M_NEGF, _M_NEGB, _M_STRICTF, _M_STRICTB, _M_SBS = range(5)
N_MASKS = 5
_B_SAME, _B_SAMEB, _B_OFF0 = range(3)
N_BMASKS = _B_OFF0 + N_MERGE


def _delta_kernel(alog_ref, dtb_ref, cwq_ref, cwk_ref, cwv_ref, nw_ref,
                  xq_ref, xk_ref, xv_ref, z_ref, bat_ref,
                  xqc_ref, xkc_ref, xvc_ref, zc_ref, batc_ref,
                  *rest, ctx_out):
    if ctx_out:
        y_ref, yc_ref = rest[:2]
        rest = rest[2:]
    else:
        y_ref, yc_ref = rest[0], None
        rest = rest[1:]
    xp, qn, kn, vn, of, ob, gates, masks, bmasks, p_buf, n_buf, qe_buf, ol_buf, tot_buf = rest
    h = pl.program_id(1)
    gsz = GROUP
    ck = A_CHUNK
    nck = gsz // ck
    dk = A_DK

    @pl.when(jnp.logical_and(pl.program_id(0) == 0, h == 0))
    def _build_masks():
        ri = lax.broadcasted_iota(jnp.int32, (gsz, gsz), 0)
        ci = lax.broadcasted_iota(jnp.int32, (gsz, gsz), 1)
        same = (ri // ck) == (ci // ck)
        masks[_M_NEGF] = jnp.where(jnp.logical_and(same, ri >= ci), 0.0, -jnp.inf)
        masks[_M_NEGB] = jnp.where(jnp.logical_and(same, ri <= ci), 0.0, -jnp.inf)
        masks[_M_STRICTF] = jnp.where(jnp.logical_and(same, ri > ci), 1.0, 0.0)
        masks[_M_STRICTB] = jnp.where(jnp.logical_and(same, ri < ci), 1.0, 0.0)
        cj = ci % ck
        masks[_M_SBS] = jnp.where(ri < ck, jnp.where(ri == cj, 1.0, 0.0),
                                  jnp.where(((ri - ck) // INV_BASE) == (cj // INV_BASE), 1.0, 0.0))
        bmasks[_B_SAME] = jnp.where(same, 1.0, 0.0).astype(BF16)
        bmasks[_B_SAMEB] = jnp.where((ri // INV_BASE) == (ci // INV_BASE), 1.0, 0.0).astype(BF16)
        for lvl in range(N_MERGE):
            m = INV_BASE << lvl
            bmasks[_B_OFF0 + lvl] = jnp.where(
                jnp.logical_and((ri // (2 * m)) == (ci // (2 * m)), (ri // m) != (ci // m)), 1.0, 0.0).astype(BF16)

    lane8 = lax.broadcasted_iota(jnp.int32, alog_ref.shape, 1)
    a_all = jnp.exp(alog_ref[...])
    dt_all = dtb_ref[...]

    def pick(v, idx):
        return jnp.sum(jnp.where(lane8 == idx, v, 0.0), axis=1, keepdims=True)

    a_f, a_b = pick(a_all, h), pick(a_all, h + A_HEADS)
    dt_f, dt_b = pick(dt_all, h), pick(dt_all, h + A_HEADS)
    nw = nw_ref[...]

    def conv_pass(x_ref, cw_ref, dst, n, l2_scale):
        zpad = jnp.zeros((CONV_PAD, dk), F32)
        xp[0:CONV_PAD, :] = zpad
        xp[CONV_PAD + n:2 * CONV_PAD + n, :] = zpad
        ng = n // gsz

        def cp(i, _):
            r0 = pl.multiple_of(i * gsz, gsz)
            xp[pl.ds(CONV_PAD + r0, gsz), :] = x_ref[pl.ds(r0, gsz), :]
            return 0
        lax.fori_loop(0, ng, cp, 0)
        w = cw_ref[...]

        def tile_chain(g):
            r0 = pl.multiple_of(g * gsz, gsz)
            acc = None
            for j in range(A_CONV):
                term = xp[pl.ds(r0 + (CONV_PAD - A_CONV // 2 + j), gsz), :] * w[j:j + 1, :]
                acc = term if acc is None else acc + term
            s = acc * jax.nn.sigmoid(acc)
            if l2_scale is not None:
                ss = jnp.sum(s * s, axis=-1, keepdims=True)
                yield
                s = s * (lax.rsqrt(ss + EPS) * l2_scale)
            dst[pl.ds(r0, gsz), :] = s.astype(BF16)

        per = CONV_TILES_PER_STEP if ng % CONV_TILES_PER_STEP == 0 else 1

        def body(i, _):
            _run_interleaved([tile_chain(i * per + j) for j in range(per)])
            return 0
        lax.fori_loop(0, ng // per, body, 0)

    def prologue(xq, xk, xv, bat, n):
        ng = n // gsz
        conv_pass(xq, cwq_ref, qn, n, dk ** -0.5)
        conv_pass(xk, cwk_ref, kn, n, 1.0)
        conv_pass(xv, cwv_ref, vn, n, None)

        beta_f = jax.nn.sigmoid(bat[pl.ds(h, 1), :])
        beta_b = jax.nn.sigmoid(bat[pl.ds(h + A_HEADS, 1), :])
        g_f = -a_f * jax.nn.softplus(bat[pl.ds(h + 2 * A_HEADS, 1), :] + dt_f)
        g_b = -a_b * jax.nn.softplus(bat[pl.ds(h + 3 * A_HEADS, 1), :] + dt_b)
        row = lax.broadcasted_iota(jnp.int32, (8, n), 0)
        pos = lax.broadcasted_iota(jnp.int32, (8, n), 1) % ck
        g2 = jnp.where(row == 0, g_f, jnp.where(row == 1, g_b, 0.0))
        cf, cb = g2, g2
        s = 1
        while s < ck:
            cf = cf + jnp.where(pos >= s, pltpu.roll(cf, s, 1), 0.0)
            cb = cb + jnp.where(pos < ck - s, pltpu.roll(cb, n - s, 1), 0.0)
            s *= 2
        tab = jnp.where(row == _R_GCF, cf[0:1], 0.0)
        tab = jnp.where(row == _R_BF, beta_f, tab)
        tab = jnp.where(row == _R_GCB, cb[1:2], tab)
        tab = jnp.where(row == _R_BB, beta_b, tab)
        tab = jnp.where(row == _R_TEF, cb[0:1] - g_f, tab)
        tab = jnp.where(row == _R_TEB, cf[1:2] - g_b, tab)
        for g in range(ng):
            gates[g] = tab[:, g * gsz:(g + 1) * gsz]

    def block_diag(m_sbs):
        return jnp.concatenate([m_sbs.astype(BF16)] * nck, axis=0) * bmasks[_B_SAME]

    def local_chain(g, slot, j, d):
        fwd = d == 0
        r0 = pl.multiple_of(g * gsz, gsz)
        kg = kn[pl.ds(r0, gsz), :]
        qg = qn[pl.ds(r0, gsz), :]
        vg = vn[pl.ds(r0, gsz), :]
        rg = gates[g]
        rgt = rg.T
        i_gc, i_b, i_te = (_R_GCF, _R_BF, _R_TEF) if fwd else (_R_GCB, _R_BB, _R_TEB)
        gc_row, b_row, te_row = rg[i_gc:i_gc + 1], rg[i_b:i_b + 1], rg[i_te:i_te + 1]
        gc_col, b_col, te_col = rgt[:, i_gc:i_gc + 1], rgt[:, i_b:i_b + 1], rgt[:, i_te:i_te + 1]
        dec = jnp.exp((gc_col - gc_row) + masks[_M_NEGF if fwd else _M_NEGB])
        ab = _dot_nt(jnp.concatenate([kg, qg], axis=0), kg)
        yield
        lm = ab[:gsz] * b_col * dec * masks[_M_STRICTF if fwd else _M_STRICTB]
        qm = (ab[gsz:] * dec).astype(BF16)
        lmb = lm.astype(BF16)
        l_sbs = lm[0:ck]
        for c in range(1, nck):
            l_sbs = l_sbs + lm[c * ck:(c + 1) * ck]
        ld_sbs = l_sbs * masks[_M_SBS, ck:2 * ck, :]
        x = masks[_M_SBS, 0:ck, :] - ld_sbs
        p = jnp.dot(ld_sbs.astype(BF16), lmb * bmasks[_B_SAMEB], preferred_element_type=F32)
        yield
        order = 2
        while order < INV_BASE:
            pbd = block_diag(p)
            if 2 * order < INV_BASE:
                r = jnp.dot(jnp.concatenate([x.astype(BF16), p.astype(BF16)], axis=0), pbd,
                            preferred_element_type=F32)
                yield
                x = x + r[:ck]
                p = r[ck:]
            else:
                r = jnp.dot(x.astype(BF16), pbd, preferred_element_type=F32)
                yield
                x = x + r
            order *= 2
        for lvl in range(N_MERGE):
            y = jnp.dot(x.astype(BF16), lmb * bmasks[_B_OFF0 + lvl], preferred_element_type=F32)
            yield
            r = jnp.dot(y.astype(BF16), block_diag(x), preferred_element_type=F32)
            yield
            x = x - r
        kgc = (kg.astype(F32) * jnp.exp(gc_col)).astype(BF16)
        wu = jnp.dot(block_diag(x * b_row), jnp.concatenate([kgc, vg], axis=1), preferred_element_type=F32)
        yield
        wu = wu.astype(BF16)
        qwu = jnp.dot(qm, wu, preferred_element_type=F32)
        kt = (kg.astype(F32) * jnp.exp(te_col)).astype(BF16)
        pn = [_dot_tn(kt[c * ck:(c + 1) * ck], wu[c * ck:(c + 1) * ck]) for c in range(nck)]
        yield
        ol_buf[slot, d, j] = qwu[:, dk:]
        qe_buf[slot, d, j] = (qg.astype(F32) * jnp.exp(gc_col) - qwu[:, :dk]).astype(BF16)
        for c in range(nck):
            p_buf[slot, d, j, c] = pn[c][:, :dk].astype(BF16)
            n_buf[slot, d, j, c] = pn[c][:, dk:]
        tot_buf[slot, d, j, 0:1, :] = jnp.exp(gc_row + te_row)

    def rec_chain(groups_fb, slot, s_f, s_b, out):
        zblk = jnp.zeros((dk, dk), BF16)
        for j, (gf, gb) in enumerate(zip(*groups_fb)):
            rf = pl.multiple_of(gf * gsz, gsz)
            rb = pl.multiple_of(gb * gsz, gsz)
            tot_f = tot_buf[slot, 0, j, 0:1, :]
            tot_b = tot_buf[slot, 1, j, 0:1, :]
            for cf in range(nck):
                cb = nck - 1 - cf
                lhs = jnp.concatenate(
                    [jnp.concatenate([p_buf[slot, 0, j, cf], qe_buf[slot, 0, j, cf * ck:(cf + 1) * ck, :]], axis=0),
                     jnp.concatenate([p_buf[slot, 1, j, cb], qe_buf[slot, 1, j, cb * ck:(cb + 1) * ck, :]], axis=0)],
                    axis=1)
                rhs = jnp.concatenate([jnp.concatenate([s_f.astype(BF16), zblk], axis=1),
                                       jnp.concatenate([zblk, s_b.astype(BF16)], axis=1)], axis=0)
                ps = jnp.dot(lhs, rhs, preferred_element_type=F32)
                yield
                of[pl.ds(rf + cf * ck, ck), :] = ol_buf[slot, 0, j, cf * ck:(cf + 1) * ck, :] + ps[dk:, :dk]
                ob[pl.ds(rb + cb * ck, ck), :] = ol_buf[slot, 1, j, cb * ck:(cb + 1) * ck, :] + ps[dk:, dk:]
                s_f = s_f * tot_f[:, cf * ck:cf * ck + 1] - ps[:dk, :dk] + n_buf[slot, 0, j, cf]
                s_b = s_b * tot_b[:, cb * ck:cb * ck + 1] - ps[:dk, dk:] + n_buf[slot, 1, j, cb]
        out[0], out[1] = s_f, s_b

    def sweep_plan(n):
        ng = n // gsz
        gps = GROUPS_PER_STEP if ng % GROUPS_PER_STEP == 0 else 1
        groups_f = lambda i: [i * gps + j for j in range(gps)]
        groups_b = lambda i: [ng - 1 - (i * gps + j) for j in range(gps)]

        def locals_of(i, slot):
            return ([local_chain(g, slot, j, 0) for j, g in enumerate(groups_f(i))]
                    + [local_chain(g, slot, j, 1) for j, g in enumerate(groups_b(i))])

        def recs_of(i, slot, s_f, s_b, out):
            return [rec_chain((groups_f(i), groups_b(i)), slot, s_f, s_b, out)]
        return locals_of, recs_of, ng // gps

    def pipelined(plan, s_f, s_b):
        locals_of, recs_of, nsteps = plan

        def body(i, carry):
            out = [None, None]
            slot = i % 2
            _run_interleaved(locals_of(i, slot) + recs_of(i - 1, 1 - slot, carry[0], carry[1], out))
            return tuple(out)
        s_f, s_b = lax.fori_loop(1, nsteps, body, (s_f, s_b))
        out = [None, None]
        _run_interleaved(recs_of(nsteps - 1, (nsteps - 1) % 2, s_f, s_b, out))
        return tuple(out)

    def epilogue(zr, yr, n):
        ng = n // gsz

        def tile_chain(g):
            r0 = pl.multiple_of(g * gsz, gsz)
            o = of[pl.ds(r0, gsz), :] + ob[pl.ds(r0, gsz), :]
            ms = jnp.mean(o * o, axis=-1, keepdims=True)
            yield
            zg = zr[pl.ds(r0, gsz), :]
            yr[pl.ds(r0, gsz), :] = (o * lax.rsqrt(ms + EPS) * nw * (zg * jax.nn.sigmoid(zg))).astype(yr.dtype)

        per = CONV_TILES_PER_STEP if ng % CONV_TILES_PER_STEP == 0 else 1

        def body(i, _):
            _run_interleaved([tile_chain(i * per + j) for j in range(per)])
            return 0
        lax.fori_loop(0, ng // per, body, 0)

    n_c, n_x = xqc_ref.shape[0], xq_ref.shape[0]
    zeros = jnp.zeros((dk, dk), F32)
    plan_c, plan_x = sweep_plan(n_c), sweep_plan(n_x)
    prologue(xqc_ref, xkc_ref, xvc_ref, batc_ref, n_c)
    if plan_c[2] == 1:
        _run_interleaved(plan_c[0](0, 1))
        prologue(xq_ref, xk_ref, xv_ref, bat_ref, n_x)
        out = [None, None]
        _run_interleaved(plan_c[1](0, 1, zeros, zeros, out) + plan_x[0](0, 0))
        s_f, s_b = out
        if ctx_out:
            epilogue(zc_ref, yc_ref, n_c)
    else:
        _run_interleaved(plan_c[0](0, 0))
        s_f, s_b = pipelined(plan_c, zeros, zeros)
        if ctx_out:
            epilogue(zc_ref, yc_ref, n_c)
        prologue(xq_ref, xk_ref, xv_ref, bat_ref, n_x)
        _run_interleaved(plan_x[0](0, 0))
    pipelined(plan_x, s_f, s_b)
    epilogue(z_ref, y_ref, n_x)


def delta_mixer(aqkv, az, bat, aqkvc, azc, batc, conv_w_l, a_log_l, dt_bias_l, norm_w_l, ctx_out):
    b, t, _ = aqkv.shape
    l = aqkvc.shape[1]
    assert t % GROUP == 0 and l % GROUP == 0
    dk = A_DK
    nh = A_HEADS
    col = lambda n, off: pl.BlockSpec((None, n, dk), lambda i, j: (i, 0, j + off))
    cw = lambda off: pl.BlockSpec((A_CONV, dk), lambda i, j: (0, j + off))
    small = pl.BlockSpec((1, 2 * nh), lambda i, j: (0, 0))
    gate = lambda n: pl.BlockSpec((None, 4 * nh, n), lambda i, j: (i, 0, 0))
    out_shape = [jax.ShapeDtypeStruct((b, t, nh * dk), BF16)]
    out_specs = [col(t, 0)]
    if ctx_out:
        out_shape.append(jax.ShapeDtypeStruct((b, l, nh * dk), BF16))
        out_specs.append(col(l, 0))
    nmax = max(t, l)
    nck = GROUP // A_CHUNK
    lead = (2, 2, GROUPS_PER_STEP)
    handover = [pltpu.VMEM(lead + (nck, dk, dk), BF16), pltpu.VMEM(lead + (nck, dk, dk), F32),
                pltpu.VMEM(lead + (GROUP, dk), BF16), pltpu.VMEM(lead + (GROUP, dk), F32),
                pltpu.VMEM(lead + (8, GROUP), F32)]
    res = pl.pallas_call(
        functools.partial(_delta_kernel, ctx_out=ctx_out),
        grid=(b, nh),
        in_specs=[small, small, cw(0), cw(nh), cw(2 * nh), pl.BlockSpec((1, dk), lambda i, j: (0, 0)),
                  col(t, 0), col(t, nh), col(t, 2 * nh), col(t, 0), gate(t),
                  col(l, 0), col(l, nh), col(l, 2 * nh), col(l, 0), gate(l)],
        out_specs=out_specs,
        out_shape=out_shape,
        scratch_shapes=[pltpu.VMEM((nmax + 2 * CONV_PAD, dk), F32),
                        pltpu.VMEM((nmax, dk), BF16), pltpu.VMEM((nmax, dk), BF16), pltpu.VMEM((nmax, dk), BF16),
                        pltpu.VMEM((nmax, dk), F32), pltpu.VMEM((nmax, dk), F32),
                        pltpu.VMEM((nmax // GROUP, 8, GROUP), F32),
                        pltpu.VMEM((N_MASKS, GROUP, GROUP), F32),
                        pltpu.VMEM((N_BMASKS, GROUP, GROUP), BF16)] + handover,
        compiler_params=_cparams(("arbitrary", "arbitrary")),
        name="delta_mixer",
    )(a_log_l.reshape(1, -1), dt_bias_l.reshape(1, -1), conv_w_l, conv_w_l, conv_w_l, norm_w_l.reshape(1, -1),
      aqkv, aqkv, aqkv, az, bat, aqkvc, aqkvc, aqkvc, azc, batc)
    return res if ctx_out else (res[0], None)


def _rope_angles(pos, n_freq):
    inv = ROPE_BASE ** (-jnp.arange(n_freq, dtype=F32) / n_freq)
    return pos[:, None] * inv[None, :]


def _position_tables(t):
    rows_n = t // GRID_W
    rows = jnp.repeat(jnp.arange(rows_n, dtype=F32), GRID_W)
    cols = jnp.tile(jnp.arange(GRID_W, dtype=F32), rows_n)
    n_ax = B_HD // 4
    ar, ac = _rope_angles(rows, n_ax), _rope_angles(cols, n_ax)
    cr, sr, cc, sc = jnp.cos(ar), jnp.sin(ar), jnp.cos(ac), jnp.sin(ac)
    zz = jnp.zeros_like(sr)
    reps = LANES // B_HD
    cosa = jnp.tile(jnp.concatenate([cr, cr, cc, cc], axis=1), (1, reps))
    sinm = jnp.tile(jnp.concatenate([-sr, zz, -sc, zz], axis=1), (1, reps))
    sinp = jnp.tile(jnp.concatenate([zz, sr, zz, sc], axis=1), (1, reps))
    at = _rope_angles(jnp.arange(t, dtype=F32), C_HD // 2)
    cosr = jnp.concatenate([jnp.cos(at), jnp.cos(at)], axis=1)
    sinr = jnp.concatenate([-jnp.sin(at), jnp.sin(at)], axis=1)
    return (cosa, sinm, sinp, cosr, sinr)


def _identity_tables(l):
    one, zero = jnp.ones((l, LANES), F32), jnp.zeros((l, LANES), F32)
    return (one, zero, zero, one, zero)


def _regroup_weights(w):
    sizes = (3 * A_WIDTH, A_WIDTH, 2 * A_HEADS, 2 * A_HEADS, B_Q_HEADS * B_HD, 2 * B_KV_HEADS * B_HD, BR_WIDTH,
             3 * BR_WIDTH, BR_WIDTH, N_BRANCH * D_MODEL)
    pts = np.cumsum(sizes)[:-1]
    aqkv, az, abeta, aalpha, bq, bkv, bz, cqkv, cz, mg = jnp.split(w.astype(BF16), pts, axis=1)
    pad = jnp.zeros((w.shape[0], LANES - abeta.shape[1] - aalpha.shape[1]), BF16)
    heads = [bkv[:, i * B_HD:(i + 1) * B_HD] for i in range(2 * B_KV_HEADS)]
    bkv2 = [hd for hd in heads for _ in range(LANES // B_HD)]
    out = jnp.concatenate([aqkv, az, abeta, aalpha, pad, bq, *bkv2, bz, cqkv, cz, mg], axis=1)
    assert out.shape[1] == W_TOTAL
    return out


def kernel(x, c, ctx, c_ctx, w_ada, b_ada, norm_w, w_in, a_conv_w, a_log, a_dt_bias, a_norm_w, b_sink, c_decay,
           c_norm_w, w_branch, w_out, final_norm_w):
    b, t, d = x.shape
    l = ctx.shape[1]
    depth = w_ada.shape[0]
    assert d == D_MODEL
    assert t % MERGE_TM == 0 and (b * l) % MERGE_TM == 0 and t % GRID_W == 0

    n_mod = -(-(b + 1) // 8) * 8
    cc = jnp.concatenate([c, c_ctx[None, :], jnp.zeros((n_mod - b - 1, d), F32)], axis=0)
    mod_all = ada_mod(cc, w_ada, b_ada)
    x_tables = _position_tables(t)
    c_tables = _identity_tables(l)

    x2d = x.reshape(b * t, d)
    c2d = ctx.reshape(b * l, d)
    for layer in range(depth):
        last = layer == depth - 1
        mod = mod_all[layer].reshape(n_mod, 1, 3 * d)
        w_all = _regroup_weights(w_in[layer])
        nw = norm_w[layer].reshape(1, d)
        px = in_proj(x2d, mod, lambda i: i // (t // _inproj_tile(t)), nw, x_tables, w_all, t)
        pc = in_proj(c2d, mod, lambda i: b, nw, c_tables, w_all, l)
        (aqkv, az, ba, bq, bk, bv, bz, cq, ck, cv, cz, mg) = [a.reshape(b, t, -1) for a in px]
        (aqkvc, azc, bac, bqc, bkc, bvc, bzc, cqc, ckc, cvc, czc, mgc) = [a.reshape(b, l, -1) for a in pc]
        bat = jnp.swapaxes(ba[:, :, :4 * A_HEADS], 1, 2)
        batc = jnp.swapaxes(bac[:, :, :4 * A_HEADS], 1, 2)

        ya, yac = delta_mixer(aqkv, az, bat, aqkvc, azc, batc, a_conv_w[layer], a_log[layer], a_dt_bias[layer],
                              a_norm_w[layer], not last)
        yb = window_attn(bq, bk, bv, bz, bkc, bvc, b_sink[layer])
        yc, ycc = retention(cq, ck, cv, cz, cqc, ckc, cvc, czc, c_decay[layer], c_norm_w[layer], not last)

        wbr = w_branch[layer].astype(BF16)
        wo = w_out[layer].astype(BF16)
        fw = final_norm_w.reshape(1, d)
        flat = lambda a: a.reshape(-1, a.shape[-1])
        x2d_new = merge_out(flat(ya), flat(yb), flat(yc), flat(mg), x2d, mod, lambda i: i // (t // MERGE_TM),
                            wbr, wo, fw, last)
        if not last:
            ybc = context_attn(bqc, bkc, bvc, bzc, b_sink[layer])
            c2d = merge_out(flat(yac), flat(ybc), flat(ycc), flat(mgc), c2d, mod, lambda i: b, wbr, wo, fw, False)
        x2d = x2d_new
    return x2d.reshape(b, t, d)
```

```python
import functools

import jax
import jax.numpy as jnp
import numpy as np
from jax import lax
from jax.experimental import pallas as pl
from jax.experimental.pallas import tpu as pltpu

F32 = jnp.float32
BF16 = jnp.bfloat16

D_MODEL = 1024
GRID_W = 64
EPS = 1e-6
ROPE_BASE = 10000.0
BR_WIDTH = D_MODEL // 2
A_DK = 128
A_HEADS = BR_WIDTH // A_DK
A_WIDTH = A_HEADS * A_DK
A_CONV = 5
A_CHUNK = 64
B_HD = 64
B_Q_HEADS = BR_WIDTH // B_HD
B_KV_HEADS = B_Q_HEADS // 4
B_BLOCK = 128
C_HD = 128
C_HEADS = BR_WIDTH // C_HD
N_BRANCH = 3

LANES = 128
GROUP = 256
VMEM_LIMIT = 56 * 1024 * 1024


def _cparams(sem):
    return pltpu.CompilerParams(dimension_semantics=sem, vmem_limit_bytes=VMEM_LIMIT)


def _const_spec(shape):
    nd = len(shape)
    return pl.BlockSpec(shape, lambda *_: (0,) * nd, pipeline_mode=pl.Buffered(1))


def _ada_kernel(c_ref, w_ref, b_ref, o_ref):
    c = c_ref[...]
    s = c * jax.nn.sigmoid(c)
    o_ref[...] = jnp.dot(s, w_ref[...], preferred_element_type=F32,
                         precision=lax.Precision.HIGHEST) + b_ref[...]


def ada_mod(cc, w_ada, b_ada):
    depth = w_ada.shape[0]
    r, d = cc.shape
    tn = 1024
    nt = (3 * d) // tn
    return pl.pallas_call(
        _ada_kernel,
        grid=(depth, nt),
        in_specs=[pl.BlockSpec((r, d), lambda l, j: (0, 0)),
                  pl.BlockSpec((None, d, tn), lambda l, j: (l, 0, j)),
                  pl.BlockSpec((None, 1, tn), lambda l, j: (l, 0, j))],
        out_specs=pl.BlockSpec((None, r, tn), lambda l, j: (l, 0, j)),
        out_shape=jax.ShapeDtypeStruct((depth, r, 3 * d), F32),
        compiler_params=_cparams(("arbitrary", "arbitrary")),
        name="ada_mod",
    )(cc, w_ada, b_ada.reshape(depth, 1, 3 * d))


def _rot_axial(x, cosa, sinm, sinp):
    w = x.shape[-1]
    reps = w // LANES
    ca = jnp.concatenate([cosa] * reps, axis=-1)
    sm = jnp.concatenate([sinm] * reps, axis=-1)
    sp = jnp.concatenate([sinp] * reps, axis=-1)
    return x * ca + pltpu.roll(x, w - 16, 1) * sm + pltpu.roll(x, 16, 1) * sp


def _rot_half128(x, cosr, sinr):
    outs = []
    for t in range(x.shape[-1] // LANES):
        xt = x[:, t * LANES:(t + 1) * LANES]
        outs.append(xt * cosr + pltpu.roll(xt, LANES // 2, 1) * sinr)
    return jnp.concatenate(outs, axis=-1)


LOG2E = 1.4426950408889634

W_SIZES = (("aqkv", 3 * A_WIDTH), ("az", A_WIDTH), ("ba", LANES), ("bq", B_Q_HEADS * B_HD),
           ("bkv", 2 * B_KV_HEADS * LANES), ("bz", BR_WIDTH), ("cqkv", 3 * BR_WIDTH), ("cz", BR_WIDTH),
           ("mg", N_BRANCH * D_MODEL))
W_OFF = {}
_off = 0
for _name, _n in W_SIZES:
    W_OFF[_name] = (_off, _off + _n)
    _off += _n
W_TOTAL = _off


def _inproj_kernel(x_ref, mod_ref, nw_ref, cosa_ref, sinm_ref, sinp_ref, cosr_ref, sinr_ref, w_ref,
                   o_aqkv, o_az, o_ba, o_bq, o_bk, o_bv, o_bz, o_cq, o_ck, o_cv, o_cz, o_mg):
    d = D_MODEL
    x = x_ref[...]
    y = x * lax.rsqrt(jnp.mean(x * x, axis=-1, keepdims=True) + EPS) * nw_ref[...]
    mod = mod_ref[...]
    h = (y * (1.0 + mod[:, d:2 * d]) + mod[:, :d]).astype(BF16)

    def proj(name):
        a, b = W_OFF[name]
        return jnp.dot(h, w_ref[:, a:b], preferred_element_type=F32)

    o_aqkv[...] = proj("aqkv")
    o_az[...] = proj("az")
    o_ba[...] = proj("ba")
    cosa, sinm, sinp = cosa_ref[...], sinm_ref[...], sinp_ref[...]
    o_bq[...] = (_rot_axial(proj("bq"), cosa, sinm, sinp) * (B_HD ** -0.5 * LOG2E)).astype(BF16)
    bkv = proj("bkv")
    half = bkv.shape[-1] // 2
    o_bk[...] = _rot_axial(bkv[:, :half], cosa, sinm, sinp).astype(BF16)
    o_bv[...] = bkv[:, half:].astype(BF16)
    o_bz[...] = proj("bz")
    cqkv = proj("cqkv")
    cw = cqkv.shape[-1] // 3
    cosr, sinr = cosr_ref[...], sinr_ref[...]
    o_cq[...] = _rot_half128(cqkv[:, :cw], cosr, sinr).astype(BF16)
    o_ck[...] = (_rot_half128(cqkv[:, cw:2 * cw], cosr, sinr) * (C_HD ** -0.5)).astype(BF16)
    o_cv[...] = cqkv[:, 2 * cw:].astype(BF16)
    o_cz[...] = proj("cz")
    o_mg[...] = proj("mg").astype(BF16)


INPROJ_TM = 512


def _inproj_tile(seq_len):
    return min(INPROJ_TM, seq_len)


def in_proj(x2d, mod, mod_row_of_tile, nw, tables, w_all, layer, seq_len):
    r, d = x2d.shape
    tm = _inproj_tile(seq_len)
    assert seq_len % tm == 0
    tiles_per_seq = seq_len // tm
    row = lambda i: (i, 0)
    tab = lambda i: (i % tiles_per_seq, 0)
    in_specs = [pl.BlockSpec((tm, d), row),
                pl.BlockSpec((None, 1, 3 * d), lambda i: (mod_row_of_tile(i), 0, 0)),
                _const_spec((1, d))]
    in_specs += [pl.BlockSpec((tm, LANES), tab) for _ in range(5)]
    in_specs += [pl.BlockSpec((None,) + w_all.shape[1:], lambda i: (layer, 0, 0), pipeline_mode=pl.Buffered(1))]
    (aqkv, az, ba, bq, bkv, bz, cqkv, cz, mg) = [n for _, n in W_SIZES]
    outs = [(aqkv, F32), (az, F32), (ba, F32), (bq, BF16), (bkv // 2, BF16), (bkv // 2, BF16), (bz, F32),
            (cqkv // 3, BF16), (cqkv // 3, BF16), (cqkv // 3, BF16), (cz, F32), (mg, BF16)]
    return pl.pallas_call(
        _inproj_kernel,
        grid=(r // tm,),
        in_specs=in_specs,
        out_specs=[pl.BlockSpec((tm, n), row) for n, _ in outs],
        out_shape=[jax.ShapeDtypeStruct((r, n), dt) for n, dt in outs],
        compiler_params=_cparams(("arbitrary",)),
        name="in_proj",
    )(x2d, mod, nw, *tables, w_all)


def _merge_kernel(ya_ref, yb_ref, yc_ref, mg_ref, x_ref, mod_ref, wbr_ref, wo_ref, fw_ref, o_ref, *, final):
    d = D_MODEL
    merged = None
    for i, y_ref in enumerate((ya_ref, yb_ref, yc_ref)):
        gate_i = jax.nn.sigmoid(mg_ref[:, i * d:(i + 1) * d].astype(F32))
        term = gate_i * jnp.dot(y_ref[...], wbr_ref[i], preferred_element_type=F32)
        merged = term if merged is None else merged + term
    out = jnp.dot(merged.astype(BF16), wo_ref[...], preferred_element_type=F32)
    xn = x_ref[...] + mod_ref[:, 2 * d:] * out
    if final:
        xn = xn * lax.rsqrt(jnp.mean(xn * xn, axis=-1, keepdims=True) + EPS) * fw_ref[...]
    o_ref[...] = xn


MERGE_TM = 512


def merge_out(ya, yb, yc, mg, x2d, mod, mod_row_of_tile, wbr, wo, fw, final):
    r, d = x2d.shape
    tm = MERGE_TM
    row = lambda i: (i, 0)
    bw = ya.shape[1]
    return pl.pallas_call(
        functools.partial(_merge_kernel, final=final),
        grid=(r // tm,),
        in_specs=[pl.BlockSpec((tm, bw), row), pl.BlockSpec((tm, bw), row), pl.BlockSpec((tm, bw), row),
                  pl.BlockSpec((tm, N_BRANCH * d), row), pl.BlockSpec((tm, d), row),
                  pl.BlockSpec((None, 1, 3 * d), lambda i: (mod_row_of_tile(i), 0, 0)),
                  _const_spec(wbr.shape), _const_spec(wo.shape), _const_spec((1, d))],
        out_specs=pl.BlockSpec((tm, d), row),
        out_shape=jax.ShapeDtypeStruct((r, d), F32),
        compiler_params=_cparams(("arbitrary",)),
        name="merge_out",
    )(ya, yb, yc, mg, x2d, mod, wbr, wo, fw)


def _dot_nt(a, b):
    return lax.dot_general(a, b, (((1,), (1,)), ((), ())), preferred_element_type=F32)


def _dot_tn(a, b):
    return lax.dot_general(a, b, (((0,), (0,)), ((), ())), preferred_element_type=F32)


def _run_interleaved(chains):
    active = list(chains)
    while active:
        for ch in list(active):
            try:
                next(ch)
            except StopIteration:
                active.remove(ch)


RET_GROUPS_PER_STEP = 4


def _retention_kernel(cd_ref, q_ref, k_ref, v_ref, z_ref, qc_ref, kc_ref, vc_ref, zc_ref, nw_ref,
                      *rest, ctx_out):
    if ctx_out:
        y_ref, yc_ref, u_scr, rhs_scr = rest
    else:
        (y_ref, u_scr, rhs_scr), yc_ref = rest, None
    h = pl.program_id(1)
    c = GROUP
    dh = q_ref.shape[-1]
    cd = cd_ref[...]
    lgv = jax.nn.log_sigmoid(cd)
    lane = lax.broadcasted_iota(jnp.int32, cd.shape, 1)
    lgf = jnp.sum(jnp.where(lane == h, lgv, 0.0), axis=1, keepdims=True)
    lgb = jnp.sum(jnp.where(lane == h + C_HEADS, lgv, 0.0), axis=1, keepdims=True)
    dij = (lax.broadcasted_iota(jnp.int32, (c, c), 0) - lax.broadcasted_iota(jnp.int32, (c, c), 1)).astype(F32)
    dmask = (jnp.where(dij >= 0, jnp.exp(jnp.maximum(dij, 0.0) * lgf), 0.0)
             + jnp.where(dij <= 0, jnp.exp(jnp.maximum(-dij, 0.0) * lgb), 0.0))
    rr = lax.broadcasted_iota(jnp.int32, (c, dh), 0).astype(F32)
    qdf = jnp.exp((rr + 1.0) * lgf)
    qdb = jnp.exp((c - rr) * lgb)
    kdf = jnp.exp((c - 1.0 - rr) * lgf)
    kdb = jnp.exp(rr * lgb)
    cdf = jnp.exp(c * lgf)
    cdb = jnp.exp(c * lgb)
    nw = nw_ref[...]

    def grouped_loop(ng, make_chain):
        per = RET_GROUPS_PER_STEP if ng % RET_GROUPS_PER_STEP == 0 else 1

        def body(i, _):
            _run_interleaved([make_chain(i * per + j) for j in range(per)])
            return 0
        lax.fori_loop(0, ng // per, body, 0)

    def state_pass(kr, vr, ng, sf, sb):
        def update_chain(g):
            r0 = pl.multiple_of(g * c, c)
            kg = kr[pl.ds(r0, c), :].astype(F32)
            lhs = jnp.concatenate([(kg * kdf).astype(BF16), (kg * kdb).astype(BF16)], axis=1)
            u = _dot_tn(lhs, vr[pl.ds(r0, c), :])
            yield
            u_scr[g] = u
        grouped_loop(ng, update_chain)

        def fbody(g, s):
            rhs_scr[g, 0:dh, :] = s.astype(BF16)
            return s * cdf + u_scr[g, 0:dh, :]
        sf = lax.fori_loop(0, ng, fbody, sf)

        def bbody(t, s):
            g = ng - 1 - t
            rhs_scr[g, dh:2 * dh, :] = s.astype(BF16)
            return s * cdb + u_scr[g, dh:2 * dh, :]
        sb = lax.fori_loop(0, ng, bbody, sb)
        return sf, sb

    def output_pass(qr, kr, vr, zr, yr, ng):
        def out_chain(g):
            r0 = pl.multiple_of(g * c, c)
            qg = qr[pl.ds(r0, c), :]
            sc = _dot_nt(qg, kr[pl.ds(r0, c), :])
            yield
            qf = qg.astype(F32)
            lhs = jnp.concatenate([(sc * dmask).astype(BF16), (qf * qdf).astype(BF16), (qf * qdb).astype(BF16)],
                                  axis=1)
            rhs = jnp.concatenate([vr[pl.ds(r0, c), :], rhs_scr[g]], axis=0)
            o = jnp.dot(lhs, rhs, preferred_element_type=F32)
            yield
            mu = jnp.mean(o, axis=-1, keepdims=True)
            oc = o - mu
            var = jnp.mean(oc * oc, axis=-1, keepdims=True)
            zg = zr[pl.ds(r0, c), :].astype(F32)
            yr[pl.ds(r0, c), :] = (oc * lax.rsqrt(var + EPS) * nw * (zg * jax.nn.sigmoid(zg))).astype(yr.dtype)
        grouped_loop(ng, out_chain)

    zeros = jnp.zeros((dh, dh), F32)
    n_c = qc_ref.shape[0] // c
    n_x = q_ref.shape[0] // c
    s_cf, s_cb = state_pass(kc_ref, vc_ref, n_c, zeros, zeros)
    if ctx_out:
        output_pass(qc_ref, kc_ref, vc_ref, zc_ref, yc_ref, n_c)
    state_pass(k_ref, v_ref, n_x, s_cf, s_cb)
    output_pass(q_ref, k_ref, v_ref, z_ref, y_ref, n_x)


def retention(cq, ck, cv, cz, cqc, ckc, cvc, czc, c_decay_l, c_norm_w_l, ctx_out):
    b, t, w = cq.shape
    l = cqc.shape[1]
    assert t % GROUP == 0 and l % GROUP == 0
    dh = C_HD
    seq = lambda n: pl.BlockSpec((None, n, dh), lambda i, j: (i, 0, j))
    out_shape = [jax.ShapeDtypeStruct((b, t, w), BF16)]
    out_specs = [seq(t)]
    if ctx_out:
        out_shape.append(jax.ShapeDtypeStruct((b, l, w), BF16))
        out_specs.append(seq(l))
    res = pl.pallas_call(
        functools.partial(_retention_kernel, ctx_out=ctx_out),
        grid=(b, C_HEADS),
        in_specs=[pl.BlockSpec((1, 2 * C_HEADS), lambda i, j: (0, 0)),
                  seq(t), seq(t), seq(t), seq(t), seq(l), seq(l), seq(l), seq(l),
                  pl.BlockSpec((1, dh), lambda i, j: (0, j))],
        out_specs=out_specs,
        out_shape=out_shape,
        scratch_shapes=[pltpu.VMEM((max(t, l) // GROUP, 2 * dh, dh), F32),
                        pltpu.VMEM((max(t, l) // GROUP, 2 * dh, dh), BF16)],
        compiler_params=_cparams(("arbitrary", "arbitrary")),
        name="retention",
    )(c_decay_l.reshape(1, -1), cq, ck, cv, cz, cqc, ckc, cvc, czc, c_norm_w_l.reshape(1, -1))
    return res if ctx_out else (res[0], None)


def _softmax_pv_chain(s, sink, vals, outs, key):
    m = jnp.maximum(jnp.max(s, axis=-1, keepdims=True), sink)
    yield
    p = jnp.exp2(s - m)
    den = jnp.sum(p, axis=-1, keepdims=True) + jnp.exp2(sink - m)
    o = jnp.dot(p.astype(BF16), vals, preferred_element_type=F32)
    yield
    outs[key] = o * pl.reciprocal(den)


def _head_lhs(q, g, lo, zero):
    grp = B_Q_HEADS // B_KV_HEADS
    rows = []
    for j in range(grp):
        t = (g * grp + j) // 2
        tile = q[:, t * LANES:(t + 1) * LANES]
        rows.append(jnp.where(lo, tile, zero) if j % 2 == 0 else jnp.where(lo, zero, tile))
    return jnp.concatenate(rows, axis=0)


WATTN_BLOCKS_PER_STEP = 4


def _wattn_kernel(sink_ref, q_ref, kp_ref, kc_ref, kn_ref, vp_ref, vc_ref, vn_ref, kx_ref, vx_ref, z_ref, y_ref):
    step = pl.program_id(1)
    nsteps = pl.num_programs(1)
    blk = B_BLOCK
    qb = WATTN_BLOCKS_PER_STEP
    l = kx_ref.shape[0]
    grp = B_Q_HEADS // B_KV_HEADS
    lo = lax.broadcasted_iota(jnp.int32, (blk, LANES), 1) < B_HD
    ri = lax.broadcasted_iota(jnp.int32, (blk, blk), 0)
    ci = lax.broadcasted_iota(jnp.int32, (blk, blk), 1)
    zero = jnp.zeros((blk, LANES), q_ref.dtype)
    kblocks = [kp_ref] + [kc_ref.at[a * blk:(a + 1) * blk] for a in range(qb)] + [kn_ref]
    vblocks = [vp_ref] + [vc_ref.at[a * blk:(a + 1) * blk] for a in range(qb)] + [vn_ref]
    for a in range(qb):
        rows = slice(a * blk, (a + 1) * blk)
        q = q_ref[rows, :]
        ok_prev = ci >= ri if a > 0 else jnp.logical_and(ci >= ri, step > 0)
        ok_next = ci <= ri if a < qb - 1 else jnp.logical_and(ci <= ri, step < nsteps - 1)
        outs = {}
        chains = []
        for g in range(B_KV_HEADS):
            gs = slice(g * LANES, (g + 1) * LANES)
            keys = jnp.concatenate([kx_ref[:, gs]] + [kblocks[a + i][:, gs] for i in range(3)], axis=0)
            vals = jnp.concatenate([vx_ref[:, gs]] + [vblocks[a + i][:, gs] for i in range(3)], axis=0)
            s_all = _dot_nt(_head_lhs(q, g, lo, zero), keys)
            for j in range(grp):
                s = s_all[j * blk:(j + 1) * blk]
                s = jnp.concatenate([s[:, :l],
                                     jnp.where(ok_prev, s[:, l:l + blk], -jnp.inf),
                                     s[:, l + blk:l + 2 * blk],
                                     jnp.where(ok_next, s[:, l + 2 * blk:], -jnp.inf)], axis=1)
                sink = jnp.full((blk, 1), sink_ref[0, g * grp + j] * LOG2E, F32)
                chains.append(_softmax_pv_chain(s, sink, vals, outs, g * grp + j))
        _run_interleaved(chains)
        tiles = [jnp.where(lo, outs[hd], outs[hd + 1]) for hd in range(0, B_Q_HEADS, 2)]
        z = z_ref[rows, :]
        y_ref[rows, :] = (jnp.concatenate(tiles, axis=1) * (z * jax.nn.sigmoid(z))).astype(y_ref.dtype)


def window_attn(bq, bk2, bv2, bz, bk2c, bv2c, sink_l):
    b, t, w = bq.shape
    l = bk2c.shape[1]
    blk = B_BLOCK
    nb = t // blk
    kw = bk2.shape[2]
    qb = WATTN_BLOCKS_PER_STEP
    assert nb % qb == 0
    prev = pl.BlockSpec((None, blk, kw), lambda i, n: (i, jnp.maximum(qb * n - 1, 0), 0))
    cur = pl.BlockSpec((None, qb * blk, kw), lambda i, n: (i, n, 0))
    nxt = pl.BlockSpec((None, blk, kw), lambda i, n: (i, jnp.minimum(qb * n + qb, nb - 1), 0))
    ctx = pl.BlockSpec((None, l, kw), lambda i, n: (i, 0, 0))
    qz = pl.BlockSpec((None, qb * blk, w), lambda i, n: (i, n, 0))
    return pl.pallas_call(
        _wattn_kernel,
        grid=(b, nb // qb),
        in_specs=[pl.BlockSpec(memory_space=pltpu.SMEM), qz, prev, cur, nxt, prev, cur, nxt, ctx, ctx, qz],
        out_specs=qz,
        out_shape=jax.ShapeDtypeStruct((b, t, w), BF16),
        compiler_params=_cparams(("arbitrary", "arbitrary")),
        name="window_attn",
    )(sink_l.reshape(1, -1), bq, bk2, bk2, bk2, bv2, bv2, bv2, bk2c, bv2c, bz)


def _cattn_kernel(sink_ref, q_ref, kx_ref, vx_ref, z_ref, y_ref):
    l = q_ref.shape[0]
    grp = B_Q_HEADS // B_KV_HEADS
    q = q_ref[...]
    lo = lax.broadcasted_iota(jnp.int32, (l, LANES), 1) < B_HD
    zero = jnp.zeros((l, LANES), q.dtype)
    outs = {}
    chains = []
    for g in range(B_KV_HEADS):
        gs = slice(g * LANES, (g + 1) * LANES)
        keys, vals = kx_ref[:, gs], vx_ref[:, gs]
        s_all = _dot_nt(_head_lhs(q, g, lo, zero), keys)
        for j in range(grp):
            sink = jnp.full((l, 1), sink_ref[0, g * grp + j] * LOG2E, F32)
            chains.append(_softmax_pv_chain(s_all[j * l:(j + 1) * l], sink, vals, outs, g * grp + j))
    _run_interleaved(chains)
    tiles = [jnp.where(lo, outs[hd], outs[hd + 1]) for hd in range(0, B_Q_HEADS, 2)]
    z = z_ref[...]
    y_ref[...] = (jnp.concatenate(tiles, axis=1) * (z * jax.nn.sigmoid(z))).astype(y_ref.dtype)


def context_attn(bqc, bk2c, bv2c, bzc, sink_l):
    b, l, w = bqc.shape
    kw = bk2c.shape[2]
    kv = pl.BlockSpec((None, l, kw), lambda i: (i, 0, 0))
    qz = pl.BlockSpec((None, l, w), lambda i: (i, 0, 0))
    return pl.pallas_call(
        _cattn_kernel,
        grid=(b,),
        in_specs=[pl.BlockSpec(memory_space=pltpu.SMEM), qz, kv, kv, qz],
        out_specs=qz,
        out_shape=jax.ShapeDtypeStruct((b, l, w), BF16),
        compiler_params=_cparams(("arbitrary",)),
        name="context_attn",
    )(sink_l.reshape(1, -1), bqc, bk2c, bv2c, bzc)


CONV_PAD = 8
GROUPS_PER_STEP = 4
CONV_TILES_PER_STEP = 4
_R_GCF, _R_BF, _R_GCB, _R_BB, _R_TEF, _R_TEB = range(6)
INV_BASE = 8
N_MERGE = int(np.log2(A_CHUNK // INV_BASE))
_M_NEGF, _M_NEGB, _M_STRICTF, _M_STRICTB, _M_SBS = range(5)
N_MASKS = 5
_B_SAME, _B_SAMEB, _B_OFF0 = range(3)
N_BMASKS = _B_OFF0 + N_MERGE


def _delta_kernel(alog_ref, dtb_ref, cwq_ref, cwk_ref, cwv_ref, nw_ref,
                  xq_ref, xk_ref, xv_ref, z_ref, bat_ref,
                  xqc_ref, xkc_ref, xvc_ref, zc_ref, batc_ref,
                  *rest, ctx_out):
    if ctx_out:
        y_ref, yc_ref = rest[:2]
        rest = rest[2:]
    else:
        y_ref, yc_ref = rest[0], None
        rest = rest[1:]
    xp, qn, kn, vn, of, ob, gates, masks, bmasks, p_buf, n_buf, qe_buf, ol_buf, tot_buf = rest
    h = pl.program_id(1)
    gsz = GROUP
    ck = A_CHUNK
    nck = gsz // ck
    dk = A_DK

    @pl.when(jnp.logical_and(pl.program_id(0) == 0, h == 0))
    def _build_masks():
        ri = lax.broadcasted_iota(jnp.int32, (gsz, gsz), 0)
        ci = lax.broadcasted_iota(jnp.int32, (gsz, gsz), 1)
        same = (ri // ck) == (ci // ck)
        masks[_M_NEGF] = jnp.where(jnp.logical_and(same, ri >= ci), 0.0, -jnp.inf)
        masks[_M_NEGB] = jnp.where(jnp.logical_and(same, ri <= ci), 0.0, -jnp.inf)
        masks[_M_STRICTF] = jnp.where(jnp.logical_and(same, ri > ci), 1.0, 0.0)
        masks[_M_STRICTB] = jnp.where(jnp.logical_and(same, ri < ci), 1.0, 0.0)
        cj = ci % ck
        masks[_M_SBS] = jnp.where(ri < ck, jnp.where(ri == cj, 1.0, 0.0),
                                  jnp.where(((ri - ck) // INV_BASE) == (cj // INV_BASE), 1.0, 0.0))
        bmasks[_B_SAME] = jnp.where(same, 1.0, 0.0).astype(BF16)
        bmasks[_B_SAMEB] = jnp.where((ri // INV_BASE) == (ci // INV_BASE), 1.0, 0.0).astype(BF16)
        for lvl in range(N_MERGE):
            m = INV_BASE << lvl
            bmasks[_B_OFF0 + lvl] = jnp.where(
                jnp.logical_and((ri // (2 * m)) == (ci // (2 * m)), (ri // m) != (ci // m)), 1.0, 0.0).astype(BF16)

    lane8 = lax.broadcasted_iota(jnp.int32, alog_ref.shape, 1)
    a_all = jnp.exp(alog_ref[...])
    dt_all = dtb_ref[...]

    def pick(v, idx):
        return jnp.sum(jnp.where(lane8 == idx, v, 0.0), axis=1, keepdims=True)

    a_f, a_b = pick(a_all, h), pick(a_all, h + A_HEADS)
    dt_f, dt_b = pick(dt_all, h), pick(dt_all, h + A_HEADS)
    nw = nw_ref[...]

    def conv_pass(x_ref, cw_ref, dst, n, l2_scale):
        zpad = jnp.zeros((CONV_PAD, dk), F32)
        xp[0:CONV_PAD, :] = zpad
        xp[CONV_PAD + n:2 * CONV_PAD + n, :] = zpad
        ng = n // gsz

        def cp(i, _):
            r0 = pl.multiple_of(i * gsz, gsz)
            xp[pl.ds(CONV_PAD + r0, gsz), :] = x_ref[pl.ds(r0, gsz), :]
            return 0
        lax.fori_loop(0, ng, cp, 0)
        w = cw_ref[...]

        def tile_chain(g):
            r0 = pl.multiple_of(g * gsz, gsz)
            acc = None
            for j in range(A_CONV):
                term = xp[pl.ds(r0 + (CONV_PAD - A_CONV // 2 + j), gsz), :] * w[j:j + 1, :]
                acc = term if acc is None else acc + term
            s = acc * jax.nn.sigmoid(acc)
            if l2_scale is not None:
                ss = jnp.sum(s * s, axis=-1, keepdims=True)
                yield
                s = s * (lax.rsqrt(ss + EPS) * l2_scale)
            dst[pl.ds(r0, gsz), :] = s.astype(BF16)

        per = CONV_TILES_PER_STEP if ng % CONV_TILES_PER_STEP == 0 else 1

        def body(i, _):
            _run_interleaved([tile_chain(i * per + j) for j in range(per)])
            return 0
        lax.fori_loop(0, ng // per, body, 0)

    def prologue(xq, xk, xv, bat, n):
        ng = n // gsz
        conv_pass(xq, cwq_ref, qn, n, dk ** -0.5)
        conv_pass(xk, cwk_ref, kn, n, 1.0)
        conv_pass(xv, cwv_ref, vn, n, None)

        beta_f = jax.nn.sigmoid(bat[pl.ds(h, 1), :])
        beta_b = jax.nn.sigmoid(bat[pl.ds(h + A_HEADS, 1), :])
        g_f = -a_f * jax.nn.softplus(bat[pl.ds(h + 2 * A_HEADS, 1), :] + dt_f)
        g_b = -a_b * jax.nn.softplus(bat[pl.ds(h + 3 * A_HEADS, 1), :] + dt_b)
        row = lax.broadcasted_iota(jnp.int32, (8, n), 0)
        pos = lax.broadcasted_iota(jnp.int32, (8, n), 1) % ck
        g2 = jnp.where(row == 0, g_f, jnp.where(row == 1, g_b, 0.0))
        cf, cb = g2, g2
        s = 1
        while s < ck:
            cf = cf + jnp.where(pos >= s, pltpu.roll(cf, s, 1), 0.0)
            cb = cb + jnp.where(pos < ck - s, pltpu.roll(cb, n - s, 1), 0.0)
            s *= 2
        tab = jnp.where(row == _R_GCF, cf[0:1], 0.0)
        tab = jnp.where(row == _R_BF, beta_f, tab)
        tab = jnp.where(row == _R_GCB, cb[1:2], tab)
        tab = jnp.where(row == _R_BB, beta_b, tab)
        tab = jnp.where(row == _R_TEF, cb[0:1] - g_f, tab)
        tab = jnp.where(row == _R_TEB, cf[1:2] - g_b, tab)
        for g in range(ng):
            gates[g] = tab[:, g * gsz:(g + 1) * gsz]

    def block_diag(m_sbs):
        return jnp.concatenate([m_sbs.astype(BF16)] * nck, axis=0) * bmasks[_B_SAME]

    def local_chain(g, slot, j, d):
        fwd = d == 0
        r0 = pl.multiple_of(g * gsz, gsz)
        kg = kn[pl.ds(r0, gsz), :]
        qg = qn[pl.ds(r0, gsz), :]
        vg = vn[pl.ds(r0, gsz), :]
        rg = gates[g]
        rgt = rg.T
        i_gc, i_b, i_te = (_R_GCF, _R_BF, _R_TEF) if fwd else (_R_GCB, _R_BB, _R_TEB)
        gc_row, b_row, te_row = rg[i_gc:i_gc + 1], rg[i_b:i_b + 1], rg[i_te:i_te + 1]
        gc_col, b_col, te_col = rgt[:, i_gc:i_gc + 1], rgt[:, i_b:i_b + 1], rgt[:, i_te:i_te + 1]
        dec = jnp.exp((gc_col - gc_row) + masks[_M_NEGF if fwd else _M_NEGB])
        ab = _dot_nt(jnp.concatenate([kg, qg], axis=0), kg)
        yield
        lm = ab[:gsz] * b_col * dec * masks[_M_STRICTF if fwd else _M_STRICTB]
        qm = (ab[gsz:] * dec).astype(BF16)
        lmb = lm.astype(BF16)
        l_sbs = lm[0:ck]
        for c in range(1, nck):
            l_sbs = l_sbs + lm[c * ck:(c + 1) * ck]
        ld_sbs = l_sbs * masks[_M_SBS, ck:2 * ck, :]
        x = masks[_M_SBS, 0:ck, :] - ld_sbs
        p = jnp.dot(ld_sbs.astype(BF16), lmb * bmasks[_B_SAMEB], preferred_element_type=F32)
        yield
        order = 2
        while order < INV_BASE:
            pbd = block_diag(p)
            if 2 * order < INV_BASE:
                r = jnp.dot(jnp.concatenate([x.astype(BF16), p.astype(BF16)], axis=0), pbd,
                            preferred_element_type=F32)
                yield
                x = x + r[:ck]
                p = r[ck:]
            else:
                r = jnp.dot(x.astype(BF16), pbd, preferred_element_type=F32)
                yield
                x = x + r
            order *= 2
        for lvl in range(N_MERGE):
            y = jnp.dot(x.astype(BF16), lmb * bmasks[_B_OFF0 + lvl], preferred_element_type=F32)
            yield
            r = jnp.dot(y.astype(BF16), block_diag(x), preferred_element_type=F32)
            yield
            x = x - r
        kgc = (kg.astype(F32) * jnp.exp(gc_col)).astype(BF16)
        wu = jnp.dot(block_diag(x * b_row), jnp.concatenate([kgc, vg], axis=1), preferred_element_type=F32)
        yield
        wu = wu.astype(BF16)
        qwu = jnp.dot(qm, wu, preferred_element_type=F32)
        kt = (kg.astype(F32) * jnp.exp(te_col)).astype(BF16)
        pn = [_dot_tn(kt[c * ck:(c + 1) * ck], wu[c * ck:(c + 1) * ck]) for c in range(nck)]
        yield
        ol_buf[slot, d, j] = qwu[:, dk:]
        qe_buf[slot, d, j] = (qg.astype(F32) * jnp.exp(gc_col) - qwu[:, :dk]).astype(BF16)
        for c in range(nck):
            p_buf[slot, d, j, c] = pn[c][:, :dk].astype(BF16)
            n_buf[slot, d, j, c] = pn[c][:, dk:]
        tot_buf[slot, d, j, 0:1, :] = jnp.exp(gc_row + te_row)

    def rec_chain(groups_fb, slot, s_f, s_b, out):
        zblk = jnp.zeros((dk, dk), BF16)
        for j, (gf, gb) in enumerate(zip(*groups_fb)):
            rf = pl.multiple_of(gf * gsz, gsz)
            rb = pl.multiple_of(gb * gsz, gsz)
            tot_f = tot_buf[slot, 0, j, 0:1, :]
            tot_b = tot_buf[slot, 1, j, 0:1, :]
            for cf in range(nck):
                cb = nck - 1 - cf
                lhs = jnp.concatenate(
                    [jnp.concatenate([p_buf[slot, 0, j, cf], qe_buf[slot, 0, j, cf * ck:(cf + 1) * ck, :]], axis=0),
                     jnp.concatenate([p_buf[slot, 1, j, cb], qe_buf[slot, 1, j, cb * ck:(cb + 1) * ck, :]], axis=0)],
                    axis=1)
                rhs = jnp.concatenate([jnp.concatenate([s_f.astype(BF16), zblk], axis=1),
                                       jnp.concatenate([zblk, s_b.astype(BF16)], axis=1)], axis=0)
                ps = jnp.dot(lhs, rhs, preferred_element_type=F32)
                yield
                of[pl.ds(rf + cf * ck, ck), :] = ol_buf[slot, 0, j, cf * ck:(cf + 1) * ck, :] + ps[dk:, :dk]
                ob[pl.ds(rb + cb * ck, ck), :] = ol_buf[slot, 1, j, cb * ck:(cb + 1) * ck, :] + ps[dk:, dk:]
                s_f = s_f * tot_f[:, cf * ck:cf * ck + 1] - ps[:dk, :dk] + n_buf[slot, 0, j, cf]
                s_b = s_b * tot_b[:, cb * ck:cb * ck + 1] - ps[:dk, dk:] + n_buf[slot, 1, j, cb]
        out[0], out[1] = s_f, s_b

    def sweep_plan(n):
        ng = n // gsz
        gps = GROUPS_PER_STEP if ng % GROUPS_PER_STEP == 0 else 1
        groups_f = lambda i: [i * gps + j for j in range(gps)]
        groups_b = lambda i: [ng - 1 - (i * gps + j) for j in range(gps)]

        def locals_of(i, slot):
            return ([local_chain(g, slot, j, 0) for j, g in enumerate(groups_f(i))]
                    + [local_chain(g, slot, j, 1) for j, g in enumerate(groups_b(i))])

        def recs_of(i, slot, s_f, s_b, out):
            return [rec_chain((groups_f(i), groups_b(i)), slot, s_f, s_b, out)]
        return locals_of, recs_of, ng // gps

    def pipelined(plan, s_f, s_b):
        locals_of, recs_of, nsteps = plan

        def body(i, carry):
            out = [None, None]
            slot = i % 2
            _run_interleaved(locals_of(i, slot) + recs_of(i - 1, 1 - slot, carry[0], carry[1], out))
            return tuple(out)
        s_f, s_b = lax.fori_loop(1, nsteps, body, (s_f, s_b))
        out = [None, None]
        _run_interleaved(recs_of(nsteps - 1, (nsteps - 1) % 2, s_f, s_b, out))
        return tuple(out)

    def epilogue(zr, yr, n):
        ng = n // gsz

        def tile_chain(g):
            r0 = pl.multiple_of(g * gsz, gsz)
            o = of[pl.ds(r0, gsz), :] + ob[pl.ds(r0, gsz), :]
            ms = jnp.mean(o * o, axis=-1, keepdims=True)
            yield
            zg = zr[pl.ds(r0, gsz), :]
            yr[pl.ds(r0, gsz), :] = (o * lax.rsqrt(ms + EPS) * nw * (zg * jax.nn.sigmoid(zg))).astype(yr.dtype)

        per = CONV_TILES_PER_STEP if ng % CONV_TILES_PER_STEP == 0 else 1

        def body(i, _):
            _run_interleaved([tile_chain(i * per + j) for j in range(per)])
            return 0
        lax.fori_loop(0, ng // per, body, 0)

    n_c, n_x = xqc_ref.shape[0], xq_ref.shape[0]
    zeros = jnp.zeros((dk, dk), F32)
    plan_c, plan_x = sweep_plan(n_c), sweep_plan(n_x)
    prologue(xqc_ref, xkc_ref, xvc_ref, batc_ref, n_c)
    if plan_c[2] == 1:
        _run_interleaved(plan_c[0](0, 1))
        prologue(xq_ref, xk_ref, xv_ref, bat_ref, n_x)
        out = [None, None]
        _run_interleaved(plan_c[1](0, 1, zeros, zeros, out) + plan_x[0](0, 0))
        s_f, s_b = out
        if ctx_out:
            epilogue(zc_ref, yc_ref, n_c)
    else:
        _run_interleaved(plan_c[0](0, 0))
        s_f, s_b = pipelined(plan_c, zeros, zeros)
        if ctx_out:
            epilogue(zc_ref, yc_ref, n_c)
        prologue(xq_ref, xk_ref, xv_ref, bat_ref, n_x)
        _run_interleaved(plan_x[0](0, 0))
    pipelined(plan_x, s_f, s_b)
    epilogue(z_ref, y_ref, n_x)


def delta_mixer(aqkv, az, bat, aqkvc, azc, batc, conv_w_l, a_log_l, dt_bias_l, norm_w_l, ctx_out):
    b, t, _ = aqkv.shape
    l = aqkvc.shape[1]
    assert t % GROUP == 0 and l % GROUP == 0
    dk = A_DK
    nh = A_HEADS
    col = lambda n, off: pl.BlockSpec((None, n, dk), lambda i, j: (i, 0, j + off))
    cw = lambda off: pl.BlockSpec((A_CONV, dk), lambda i, j: (0, j + off))
    small = pl.BlockSpec((1, 2 * nh), lambda i, j: (0, 0))
    gate = lambda n: pl.BlockSpec((None, 4 * nh, n), lambda i, j: (i, 0, 0))
    out_shape = [jax.ShapeDtypeStruct((b, t, nh * dk), BF16)]
    out_specs = [col(t, 0)]
    if ctx_out:
        out_shape.append(jax.ShapeDtypeStruct((b, l, nh * dk), BF16))
        out_specs.append(col(l, 0))
    nmax = max(t, l)
    nck = GROUP // A_CHUNK
    lead = (2, 2, GROUPS_PER_STEP)
    handover = [pltpu.VMEM(lead + (nck, dk, dk), BF16), pltpu.VMEM(lead + (nck, dk, dk), F32),
                pltpu.VMEM(lead + (GROUP, dk), BF16), pltpu.VMEM(lead + (GROUP, dk), F32),
                pltpu.VMEM(lead + (8, GROUP), F32)]
    res = pl.pallas_call(
        functools.partial(_delta_kernel, ctx_out=ctx_out),
        grid=(b, nh),
        in_specs=[small, small, cw(0), cw(nh), cw(2 * nh), pl.BlockSpec((1, dk), lambda i, j: (0, 0)),
                  col(t, 0), col(t, nh), col(t, 2 * nh), col(t, 0), gate(t),
                  col(l, 0), col(l, nh), col(l, 2 * nh), col(l, 0), gate(l)],
        out_specs=out_specs,
        out_shape=out_shape,
        scratch_shapes=[pltpu.VMEM((nmax + 2 * CONV_PAD, dk), F32),
                        pltpu.VMEM((nmax, dk), BF16), pltpu.VMEM((nmax, dk), BF16), pltpu.VMEM((nmax, dk), BF16),
                        pltpu.VMEM((nmax, dk), F32), pltpu.VMEM((nmax, dk), F32),
                        pltpu.VMEM((nmax // GROUP, 8, GROUP), F32),
                        pltpu.VMEM((N_MASKS, GROUP, GROUP), F32),
                        pltpu.VMEM((N_BMASKS, GROUP, GROUP), BF16)] + handover,
        compiler_params=_cparams(("arbitrary", "arbitrary")),
        name="delta_mixer",
    )(a_log_l.reshape(1, -1), dt_bias_l.reshape(1, -1), conv_w_l, conv_w_l, conv_w_l, norm_w_l.reshape(1, -1),
      aqkv, aqkv, aqkv, az, bat, aqkvc, aqkvc, aqkvc, azc, batc)
    return res if ctx_out else (res[0], None)


def _rope_angles(pos, n_freq):
    inv = ROPE_BASE ** (-jnp.arange(n_freq, dtype=F32) / n_freq)
    return pos[:, None] * inv[None, :]


def _position_tables(t):
    rows_n = t // GRID_W
    rows = jnp.repeat(jnp.arange(rows_n, dtype=F32), GRID_W)
    cols = jnp.tile(jnp.arange(GRID_W, dtype=F32), rows_n)
    n_ax = B_HD // 4
    ar, ac = _rope_angles(rows, n_ax), _rope_angles(cols, n_ax)
    cr, sr, cc, sc = jnp.cos(ar), jnp.sin(ar), jnp.cos(ac), jnp.sin(ac)
    zz = jnp.zeros_like(sr)
    reps = LANES // B_HD
    cosa = jnp.tile(jnp.concatenate([cr, cr, cc, cc], axis=1), (1, reps))
    sinm = jnp.tile(jnp.concatenate([-sr, zz, -sc, zz], axis=1), (1, reps))
    sinp = jnp.tile(jnp.concatenate([zz, sr, zz, sc], axis=1), (1, reps))
    at = _rope_angles(jnp.arange(t, dtype=F32), C_HD // 2)
    cosr = jnp.concatenate([jnp.cos(at), jnp.cos(at)], axis=1)
    sinr = jnp.concatenate([-jnp.sin(at), jnp.sin(at)], axis=1)
    return (cosa, sinm, sinp, cosr, sinr)


def _identity_tables(l):
    one, zero = jnp.ones((l, LANES), F32), jnp.zeros((l, LANES), F32)
    return (one, zero, zero, one, zero)


def _regroup_moves():
    sizes = (3 * A_WIDTH, A_WIDTH, 2 * A_HEADS, 2 * A_HEADS, B_Q_HEADS * B_HD, 2 * B_KV_HEADS * B_HD, BR_WIDTH,
             3 * BR_WIDTH, BR_WIDTH, N_BRANCH * D_MODEL)
    src = dict(zip(("aqkv", "az", "abeta", "aalpha", "bq", "bkv", "bz", "cqkv", "cz", "mg"),
                   np.concatenate([[0], np.cumsum(sizes)[:-1]]).tolist()))
    moves = [(src[n], W_OFF[n][0], W_OFF[n][1] - W_OFF[n][0]) for n in ("aqkv", "az", "bq", "bz", "cqkv", "cz", "mg")]
    moves.append((src["abeta"], W_OFF["ba"][0], 4 * A_HEADS))
    for i in range(2 * B_KV_HEADS):
        for rep in range(LANES // B_HD):
            moves.append((src["bkv"] + i * B_HD, W_OFF["bkv"][0] + i * LANES + rep * B_HD, B_HD))
    return moves, int(sum(sizes))


def _regroup_kernel(w_ref, o_ref):
    moves, _ = _regroup_moves()
    pad_a = W_OFF["ba"][0] + 4 * A_HEADS
    o_ref[:, pad_a:W_OFF["ba"][1]] = jnp.zeros((o_ref.shape[0], W_OFF["ba"][1] - pad_a), o_ref.dtype)
    for s0, d0, n in moves:
        o_ref[:, d0:d0 + n] = w_ref[:, s0:s0 + n].astype(o_ref.dtype)


REGROUP_ROWS = 128


def _regroup_weights(w_in):
    depth, d, width = w_in.shape
    assert width == _regroup_moves()[1] and d % REGROUP_ROWS == 0
    return pl.pallas_call(
        _regroup_kernel,
        grid=(depth, d // REGROUP_ROWS),
        in_specs=[pl.BlockSpec((None, REGROUP_ROWS, width), lambda l, i: (l, i, 0))],
        out_specs=pl.BlockSpec((None, REGROUP_ROWS, W_TOTAL), lambda l, i: (l, i, 0)),
        out_shape=jax.ShapeDtypeStruct((depth, d, W_TOTAL), BF16),
        compiler_params=_cparams(("arbitrary", "arbitrary")),
        name="regroup_weights",
    )(w_in)


def kernel(x, c, ctx, c_ctx, w_ada, b_ada, norm_w, w_in, a_conv_w, a_log, a_dt_bias, a_norm_w, b_sink, c_decay,
           c_norm_w, w_branch, w_out, final_norm_w):
    b, t, d = x.shape
    l = ctx.shape[1]
    depth = w_ada.shape[0]
    assert d == D_MODEL
    assert t % MERGE_TM == 0 and (b * l) % MERGE_TM == 0 and t % GRID_W == 0

    n_mod = -(-(b + 1) // 8) * 8
    cc = jnp.concatenate([c, c_ctx[None, :], jnp.zeros((n_mod - b - 1, d), F32)], axis=0)
    mod_all = ada_mod(cc, w_ada, b_ada)
    x_tables = _position_tables(t)
    c_tables = _identity_tables(l)
    w_regrouped = _regroup_weights(w_in)

    x2d = x.reshape(b * t, d)
    c2d = ctx.reshape(b * l, d)
    for layer in range(depth):
        last = layer == depth - 1
        mod = mod_all[layer].reshape(n_mod, 1, 3 * d)
        nw = norm_w[layer].reshape(1, d)
        px = in_proj(x2d, mod, lambda i: i // (t // _inproj_tile(t)), nw, x_tables, w_regrouped, layer, t)
        pc = in_proj(c2d, mod, lambda i: b, nw, c_tables, w_regrouped, layer, l)
        (aqkv, az, ba, bq, bk, bv, bz, cq, ck, cv, cz, mg) = [a.reshape(b, t, -1) for a in px]
        (aqkvc, azc, bac, bqc, bkc, bvc, bzc, cqc, ckc, cvc, czc, mgc) = [a.reshape(b, l, -1) for a in pc]
        bat = jnp.swapaxes(ba[:, :, :4 * A_HEADS], 1, 2)
        batc = jnp.swapaxes(bac[:, :, :4 * A_HEADS], 1, 2)

        ya, yac = delta_mixer(aqkv, az, bat, aqkvc, azc, batc, a_conv_w[layer], a_log[layer], a_dt_bias[layer],
                              a_norm_w[layer], not last)
        yb = window_attn(bq, bk, bv, bz, bkc, bvc, b_sink[layer])
        yc, ycc = retention(cq, ck, cv, cz, cqc, ckc, cvc, czc, c_decay[layer], c_norm_w[layer], not last)

        wbr = w_branch[layer].astype(BF16)
        wo = w_out[layer].astype(BF16)
        fw = final_norm_w.reshape(1, d)
        flat = lambda a: a.reshape(-1, a.shape[-1])
        x2d_new = merge_out(flat(ya), flat(yb), flat(yc), flat(mg), x2d, mod, lambda i: i // (t // MERGE_TM),
                            wbr, wo, fw, last)
        if not last:
            ybc = context_attn(bqc, bkc, bvc, bzc, b_sink[layer])
            c2d = merge_out(flat(yac), flat(ybc), flat(ycc), flat(mgc), c2d, mod, lambda i: b, wbr, wo, fw, False)
        x2d = x2d_new
    return x2d.reshape(b, t, d)
```
